```python
import math
import jax
import jax.numpy as jnp
from jax import lax
import numpy as np

D_MODEL = 1024
BATCH = 16
SEQ = 256
DEPTH = 4
DEC_BATCH = 2
DEC_SEQ = 4096
PAST_LEN = 512

GRID_W = 64
N_MIXERS = 3
S5_GROUP = 16
S5_GROUPS = D_MODEL // S5_GROUP
S5_STATE = 64
NA_HEADS = 16
NA_HEAD_DIM = D_MODEL // NA_HEADS
NA_WIN_R = 8
NA_WIN_C = 16
DA_HEADS = 8
DA_HEAD_DIM = D_MODEL // (2 * DA_HEADS)
ROPE_BASE = 10000.0
Q_BLOCK = 128
D_FF = 2816
N_EXPERTS = 8
TOP_K = 2
D_FF_EXPERT = 3584
NORM_EPS = 1e-6
NEG_INF = -1e30
N_A = (DEPTH + 2) // 3
N_B = (DEPTH + 1) // 3
N_C = DEPTH // 3
N_DENSE = (DEPTH + 1) // 2
N_MOE = DEPTH // 2
F32 = jnp.float32

kernel_name = 'hybrid_s5_natten_diffattn_prefix_dit_step'


def _rmsnorm(x, g):
    xf = x.astype(F32)
    y = xf * lax.rsqrt(jnp.mean(xf * xf, axis=-1, keepdims=True) + NORM_EPS)
    return (y * g.astype(F32)).astype(x.dtype)


def _modulation(cond, w, b):
    m = jnp.einsum('nd,de->ne', jax.nn.silu(cond), w) + b
    return [t[:, None, :] for t in jnp.split(m, 6, axis=-1)]


def _pre(x, g, shift, scale):
    return _rmsnorm(x, g) * (1 + scale) + shift


def _linear_combine(e1, e2):
    a1, b1 = e1
    a2, b2 = e2
    return a1 * a2, a2 * b1 + b2


def _s5_discrete(a_re, a_im, b_re, b_im, log_step):
    lam = lax.complex(a_re.astype(F32), a_im.astype(F32))
    step = jnp.exp(log_step.astype(F32))[:, None]
    lam_bar = jnp.exp(lam * step)
    b = lax.complex(b_re.astype(F32), b_im.astype(F32))
    b_bar = ((lam_bar - 1.0) / lam)[..., None] * b
    return lam_bar, b_bar


def _s5_scan(u, lam_bar, b_bar, h0, reverse):
    bu = jnp.einsum('btgj,gpj->btgp', u, b_bar)
    if h0 is not None:
        edge = -1 if reverse else 0
        bu = bu.at[:, edge].add(lam_bar * h0)
    a = jnp.broadcast_to(lam_bar, bu.shape)
    _, hs = lax.associative_scan(_linear_combine, (a, bu), axis=1, reverse=reverse)
    return hs


def _s5_mixer(h, p, h0_re=None, h0_im=None):
    a_re, a_im, b_re, b_im, c_re, c_im, log_step, d_skip, w_glu_a, w_glu_b = p
    nb, t, _ = h.shape
    u = h.astype(F32).reshape(nb, t, S5_GROUPS, S5_GROUP)
    y = d_skip.astype(F32).reshape(S5_GROUPS, S5_GROUP) * u
    finals = []
    for d in range(2):
        lam_bar, b_bar = _s5_discrete(a_re[d], a_im[d], b_re[d], b_im[d], log_step[d])
        h0 = None if h0_re is None else lax.complex(h0_re[:, d].astype(F32), h0_im[:, d].astype(F32))
        hs = _s5_scan(u, lam_bar, b_bar, h0, reverse=(d == 1))
        cm = lax.complex(c_re[d].astype(F32), c_im[d].astype(F32))
        y = y + jnp.einsum('btgp,gjp->btgj', hs, cm).real
        if h0_re is None:
            finals.append(hs[:, 0] if d == 1 else hs[:, -1])
    y = jax.nn.gelu(y.reshape(nb, t, D_MODEL)).astype(h.dtype)
    out = (y @ w_glu_a) * jax.nn.sigmoid(y @ w_glu_b)
    if h0_re is None:
        fin = jnp.stack(finals, axis=1)
        return out, fin.real, fin.imag
    return out


def _block_attention(q, k, v):
    nb, tq, nh, dh = q.shape
    scale = dh ** -0.5
    qb = jnp.moveaxis(q.reshape(nb, tq // Q_BLOCK, Q_BLOCK, nh, dh), 1, 0)

    def one(qi):
        s = jnp.einsum('bqhd,bkhd->bhqk', qi, k).astype(F32) * scale
        pr = jax.nn.softmax(s, axis=-1).astype(v.dtype)
        return jnp.einsum('bhqk,bkhd->bqhd', pr, v)

    o = lax.map(one, qb)
    return jnp.moveaxis(o, 0, 1).reshape(nb, tq, nh, dh)


def _na_qkv(h, w_qkv):
    nb, t, _ = h.shape
    qkv = (h @ w_qkv).reshape(nb, t, 3, NA_HEADS, NA_HEAD_DIM)
    return qkv[:, :, 0], qkv[:, :, 1], qkv[:, :, 2]


def _na_context(h, w_qkv, w_o):
    q, k, v = _na_qkv(h, w_qkv)
    o = _block_attention(q, k, v)
    return o.reshape(h.shape) @ w_o, k, v


def _na_latent(h, w_qkv, w_o, rel_bias, k_ctx, v_ctx):
    nb, t, _ = h.shape
    rows = t // GRID_W
    kr = min(NA_WIN_R, rows)
    q, k, v = _na_qkv(h, w_qkv)
    scale = NA_HEAD_DIM ** -0.5
    r = np.arange(rows)
    cc = np.arange(GRID_W)
    row_idx = np.clip(r - kr // 2, 0, rows - kr)[:, None] + np.arange(kr)[None, :]
    col_start = np.clip(cc - NA_WIN_C // 2, 0, GRID_W - NA_WIN_C)
    col_ok = (cc[None, :] >= col_start[:, None]) & (cc[None, :] < col_start[:, None] + NA_WIN_C)
    d_row = row_idx - r[:, None] + NA_WIN_R - 1
    d_col = np.clip(cc[None, :] - cc[:, None], 1 - NA_WIN_C, NA_WIN_C - 1) + NA_WIN_C - 1
    bias = jnp.transpose(rel_bias[:, d_row][:, :, :, d_col], (0, 1, 3, 2, 4))
    qg = q.reshape(nb, rows, GRID_W, NA_HEADS, NA_HEAD_DIM)
    kb = k.reshape(nb, rows, GRID_W, NA_HEADS, NA_HEAD_DIM)[:, row_idx]
    vb = v.reshape(nb, rows, GRID_W, NA_HEADS, NA_HEAD_DIM)[:, row_idx]
    s_loc = jnp.einsum('brchd,brivhd->bhrciv', qg, kb).astype(F32) * scale + bias.astype(F32)
    s_loc = jnp.where(col_ok[:, None, :], s_loc, NEG_INF)
    s_ctx = jnp.einsum('brchd,blhd->bhrcl', qg, k_ctx).astype(F32) * scale
    n_loc = kr * GRID_W
    s = jnp.concatenate([s_loc.reshape(nb, NA_HEADS, rows, GRID_W, n_loc), s_ctx], axis=-1)
    pr = jax.nn.softmax(s, axis=-1).astype(v.dtype)
    p_loc = pr[..., :n_loc].reshape(nb, NA_HEADS, rows, GRID_W, kr, GRID_W)
    o = (jnp.einsum('bhrciv,brivhd->brchd', p_loc, vb)
         + jnp.einsum('bhrcl,blhd->brchd', pr[..., n_loc:], v_ctx))
    return o.reshape(nb, t, D_MODEL) @ w_o


def _axial_rope_tables(t):
    pos = jnp.arange(t)
    row = (pos // GRID_W).astype(F32)
    col = (pos % GRID_W).astype(F32)
    n_freq = DA_HEAD_DIM // 4
    inv = jnp.power(ROPE_BASE, -jnp.arange(n_freq, dtype=F32) / n_freq)
    ar = row[:, None] * inv
    ac = col[:, None] * inv
    cos = jnp.concatenate([jnp.cos(ar), jnp.cos(ar), jnp.cos(ac), jnp.cos(ac)], axis=-1)
    sin = jnp.concatenate([jnp.sin(ar), jnp.sin(ar), jnp.sin(ac), jnp.sin(ac)], axis=-1)
    return cos, sin


def _apply_rope(x, cos, sin):
    xf = x.astype(F32)
    x1, x2, x3, x4 = jnp.split(xf, 4, axis=-1)
    rot = jnp.concatenate([-x2, x1, -x4, x3], axis=-1)
    cb = cos[None, :, None, None, :]
    sb = sin[None, :, None, None, :]
    return (xf * cb + rot * sb).astype(x.dtype)


def _da_qkv(h, w_qkv):
    nb, t, _ = h.shape
    q, k, v = jnp.split(h @ w_qkv, 3, axis=-1)
    q = q.reshape(nb, t, DA_HEADS, 2, DA_HEAD_DIM)
    k = k.reshape(nb, t, DA_HEADS, 2, DA_HEAD_DIM)
    v = v.reshape(nb, t, DA_HEADS, 2 * DA_HEAD_DIM)
    return q, k, v


def _da_lambda(lq1, lk1, lq2, lk2, lam_init):
    e1 = jnp.exp(jnp.sum(lq1.astype(F32) * lk1.astype(F32)))
    e2 = jnp.exp(jnp.sum(lq2.astype(F32) * lk2.astype(F32)))
    return e1 - e2 + lam_init


def _diff_attention(q, k, v, lam):
    nb, tq = q.shape[:2]
    scale = DA_HEAD_DIM ** -0.5
    qb = jnp.moveaxis(q.reshape(nb, tq // Q_BLOCK, Q_BLOCK, DA_HEADS, 2, DA_HEAD_DIM), 1, 0)

    def one(qi):
        s = jnp.einsum('bqhmd,bkhmd->bhmqk', qi, k).astype(F32) * scale
        pr = jax.nn.softmax(s, axis=-1)
        a = pr[:, :, 0] - lam * pr[:, :, 1]
        return jnp.einsum('bhqk,bkhe->bqhe', a.astype(v.dtype), v)

    o = lax.map(one, qb)
    return jnp.moveaxis(o, 0, 1).reshape(nb, tq, DA_HEADS, 2 * DA_HEAD_DIM)


def _da_out(o, g_sub, lam_init, w_o):
    o = _rmsnorm(o, g_sub) * (1.0 - lam_init)
    return o.reshape(o.shape[0], o.shape[1], D_MODEL) @ w_o


def _da_context(h, w_qkv, w_o, lam, g_sub, lam_init):
    q, k, v = _da_qkv(h, w_qkv)
    o = _diff_attention(q, k, v, lam)
    return _da_out(o, g_sub, lam_init, w_o), k, v


def _da_latent(h, w_qkv, w_o, lam, g_sub, lam_init, k_ctx, v_ctx, cos, sin):
    q, k, v = _da_qkv(h, w_qkv)
    q = _apply_rope(q, cos, sin)
    k = _apply_rope(k, cos, sin)
    k_all = jnp.concatenate([k, k_ctx.astype(k.dtype)], axis=1)
    v_all = jnp.concatenate([v, v_ctx.astype(v.dtype)], axis=1)
    o = _diff_attention(q, k_all, v_all, lam)
    return _da_out(o, g_sub, lam_init, w_o)


def _swiglu(h, wg, wu, wd):
    return (jax.nn.silu(h @ wg) * (h @ wu)) @ wd


def _moe(h, router, wg, wu, wd):
    logits = (h @ router).astype(F32)
    top_v, top_i = lax.top_k(logits, TOP_K)
    w = jax.nn.softmax(top_v, axis=-1)
    gates = jnp.sum(jax.nn.one_hot(top_i, N_EXPERTS, dtype=F32) * w[..., None], axis=-2).astype(h.dtype)
    out = jnp.zeros_like(h)
    for e in range(N_EXPERTS):
        out = out + gates[..., e:e + 1] * _swiglu(h, wg[e], wu[e], wd[e])
    return out


def setup_inputs(seed: int = 0) -> dict:
    key = jax.random.key(seed)
    keys = iter(jax.random.split(key, 48))

    def nrm(shape, std):
        return std * jax.random.normal(next(keys), shape, F32)

    D, G, P, J = D_MODEL, S5_GROUPS, S5_STATE, S5_GROUP
    inp = {}
    inp['x_prompt'] = nrm((BATCH, SEQ, D), 1.0)
    inp['x_sample'] = nrm((DEC_BATCH, DEC_SEQ, D), 1.0)
    inp['state_l0_re'] = nrm((DEC_BATCH, 2, G, P), 0.5)
    inp['state_l0_im'] = nrm((DEC_BATCH, 2, G, P), 0.5)
    inp['cache_l1_k'] = nrm((DEC_BATCH, PAST_LEN, NA_HEADS, NA_HEAD_DIM), 1.0)
    inp['cache_l1_v'] = nrm((DEC_BATCH, PAST_LEN, NA_HEADS, NA_HEAD_DIM), 1.0)
    inp['cache_l2_k'] = nrm((DEC_BATCH, PAST_LEN, DA_HEADS, 2, DA_HEAD_DIM), 1.0)
    inp['cache_l2_v'] = nrm((DEC_BATCH, PAST_LEN, DA_HEADS, 2 * DA_HEAD_DIM), 1.0)
    inp['state_l3_re'] = nrm((DEC_BATCH, 2, G, P), 0.5)
    inp['state_l3_im'] = nrm((DEC_BATCH, 2, G, P), 0.5)
    inp['c'] = nrm((DEC_BATCH, D), 1.0)
    inp['c_ctx'] = nrm((D,), 1.0)
    inp['ada_w'] = nrm((DEPTH, D, 6 * D), 0.5 * D ** -0.5)
    inp['ada_b'] = nrm((DEPTH, 6 * D), 0.01)
    inp['norm_g'] = 1.0 + nrm((DEPTH, 4, D), 0.01)
    inp['s5_a_re'] = -0.5 + nrm((N_A, 2, G, P), 0.01)
    inp['s5_a_im'] = math.pi * jnp.arange(P, dtype=F32) + nrm((N_A, 2, G, P), 0.01)
    inp['s5_b_re'] = nrm((N_A, 2, G, P, J), (2 * J) ** -0.5)
    inp['s5_b_im'] = nrm((N_A, 2, G, P, J), (2 * J) ** -0.5)
    inp['s5_c_re'] = nrm((N_A, 2, G, J, P), (2 * P) ** -0.5)
    inp['s5_c_im'] = nrm((N_A, 2, G, J, P), (2 * P) ** -0.5)
    inp['s5_log_step'] = jax.random.uniform(next(keys), (N_A, 2, G), F32, math.log(1e-3), math.log(1e-1))
    inp['s5_d'] = nrm((N_A, D), 1.0)
    inp['s5_glu_a'] = nrm((N_A, D, D), D ** -0.5)
    inp['s5_glu_b'] = nrm((N_A, D, D), D ** -0.5)
    inp['na_wqkv'] = nrm((N_B, D, 3 * D), D ** -0.5)
    inp['na_wo'] = nrm((N_B, D, D), D ** -0.5)
    inp['na_rel_bias'] = nrm((N_B, NA_HEADS, 2 * NA_WIN_R - 1, 2 * NA_WIN_C - 1), 0.5)
    inp['da_wqkv'] = nrm((N_C, D, 3 * D), D ** -0.5)
    inp['da_wo'] = nrm((N_C, D, D), D ** -0.5)
    inp['da_lq1'] = nrm((N_C, DA_HEAD_DIM), 0.1)
    inp['da_lk1'] = nrm((N_C, DA_HEAD_DIM), 0.1)
    inp['da_lq2'] = nrm((N_C, DA_HEAD_DIM), 0.1)
    inp['da_lk2'] = nrm((N_C, DA_HEAD_DIM), 0.1)
    inp['da_subln_g'] = 1.0 + nrm((N_C, 2 * DA_HEAD_DIM), 0.01)
    inp['ff_wg'] = nrm((N_DENSE, D, D_FF), D ** -0.5)
    inp['ff_wu'] = nrm((N_DENSE, D, D_FF), D ** -0.5)
    inp['ff_wd'] = nrm((N_DENSE, D_FF, D), D_FF ** -0.5)
    inp['moe_router'] = nrm((N_MOE, D, N_EXPERTS), D ** -0.5)
    inp['moe_wg'] = nrm((N_MOE, N_EXPERTS, D, D_FF_EXPERT), D ** -0.5)
    inp['moe_wu'] = nrm((N_MOE, N_EXPERTS, D, D_FF_EXPERT), D ** -0.5)
    inp['moe_wd'] = nrm((N_MOE, N_EXPERTS, D_FF_EXPERT, D), D_FF_EXPERT ** -0.5)
    return inp


def reference(x_prompt, x_sample, state_l0_re, state_l0_im, cache_l1_k, cache_l1_v,
              cache_l2_k, cache_l2_v, state_l3_re, state_l3_im, c, c_ctx,
              ada_w, ada_b, norm_g, s5_a_re, s5_a_im, s5_b_re, s5_b_im, s5_c_re, s5_c_im,
              s5_log_step, s5_d, s5_glu_a, s5_glu_b, na_wqkv, na_wo, na_rel_bias,
              da_wqkv, da_wo, da_lq1, da_lk1, da_lq2, da_lk2, da_subln_g,
              ff_wg, ff_wu, ff_wd, moe_router, moe_wg, moe_wu, moe_wd):
    caches = [(state_l0_re, state_l0_im), (cache_l1_k, cache_l1_v),
              (cache_l2_k, cache_l2_v), (state_l3_re, state_l3_im)]
    cos, sin = _axial_rope_tables(x_sample.shape[1])
    xp, xs = x_prompt, x_sample
    new_state = []
    for i in range(DEPTH):
        kind, j = i % N_MIXERS, i // N_MIXERS
        mp = _modulation(c_ctx[None, :], ada_w[i], ada_b[i])
        ms = _modulation(c, ada_w[i], ada_b[i])
        g = norm_g[i]
        hp = _pre(xp, g[0], mp[0], mp[1])
        hs = _pre(xs, g[0], ms[0], ms[1])
        if kind == 0:
            p = (s5_a_re[j], s5_a_im[j], s5_b_re[j], s5_b_im[j], s5_c_re[j], s5_c_im[j],
                 s5_log_step[j], s5_d[j], s5_glu_a[j], s5_glu_b[j])
            yp, st_re, st_im = _s5_mixer(hp, p)
            ys = _s5_mixer(hs, p, caches[i][0], caches[i][1])
            new_state += [st_re, st_im]
        elif kind == 1:
            yp, k_c, v_c = _na_context(hp, na_wqkv[j], na_wo[j])
            ys = _na_latent(hs, na_wqkv[j], na_wo[j], na_rel_bias[j], caches[i][0], caches[i][1])
            new_state += [k_c, v_c]
        else:
            lam_init = 0.8 - 0.6 * math.exp(-0.3 * i)
            lam = _da_lambda(da_lq1[j], da_lk1[j], da_lq2[j], da_lk2[j], lam_init)
            yp, k_c, v_c = _da_context(hp, da_wqkv[j], da_wo[j], lam, da_subln_g[j], lam_init)
            ys = _da_latent(hs, da_wqkv[j], da_wo[j], lam, da_subln_g[j], lam_init,
                            caches[i][0], caches[i][1], cos, sin)
            new_state += [k_c, v_c]
        xp = xp + mp[2] * _rmsnorm(yp, g[1])
        xs = xs + ms[2] * _rmsnorm(ys, g[1])
        hp = _pre(xp, g[2], mp[3], mp[4])
        hs = _pre(xs, g[2], ms[3], ms[4])
        jf = i // 2
        if i % 2 == 0:
            yp = _swiglu(hp, ff_wg[jf], ff_wu[jf], ff_wd[jf])
            ys = _swiglu(hs, ff_wg[jf], ff_wu[jf], ff_wd[jf])
        else:
            yp = _moe(hp, moe_router[jf], moe_wg[jf], moe_wu[jf], moe_wd[jf])
            ys = _moe(hs, moe_router[jf], moe_wg[jf], moe_wu[jf], moe_wd[jf])
        xp = xp + mp[5] * _rmsnorm(yp, g[3])
        xs = xs + ms[5] * _rmsnorm(ys, g[3])
    st0_re, st0_im, k1, v1, k2, v2, st3_re, st3_im = new_state
    return (xp, xs, st0_re, st0_im, k1, v1, k2, v2, st3_re, st3_im)
```

```python
import functools
import math

import numpy as np
import jax
import jax.numpy as jnp
from jax import lax
from jax.experimental import pallas as pl
from jax.experimental.pallas import tpu as pltpu

F32 = jnp.float32
BF16 = jnp.bfloat16
HIGHEST = lax.Precision.HIGHEST

D_MODEL = 1024
GRID_W = 64
S5_GROUP = 16
S5_GROUPS = D_MODEL // S5_GROUP
S5_STATE = 64
S5_PAIRS = S5_GROUPS // 2
S5_CHUNK = 16
NA_HEADS = 16
NA_WIN_R = 8
NA_WIN_C = 16
DA_HEADS = 8
HEAD_DIM = 64
ROPE_BASE = 10000.0
N_EXPERTS = 8
NORM_EPS = 1e-6
NEG_INF = -1e30
LANES = 128
ATTN_SCALE = HEAD_DIM ** -0.5

ROW_TILE = 512
NA_QROWS = 4
NA_UROWS = NA_QROWS + NA_WIN_R


def _rms(x, g):
    return x * lax.rsqrt(jnp.mean(x * x, axis=-1, keepdims=True) + NORM_EPS) * g


def _pre(x, mod, g, shift_row):
    return _rms(x, g) * (1.0 + mod[shift_row + 1:shift_row + 2, :]) + mod[shift_row:shift_row + 1, :]


def _post(x, y, mod, g, gate_row):
    return x + mod[gate_row:gate_row + 1, :] * _rms(y, g)


def _silu(x):
    return x * jax.nn.sigmoid(x)


def _row_specs(tm, rows_per_group):
    x_spec = pl.BlockSpec((tm, D_MODEL), lambda i, *_: (i, 0))
    mod_spec = pl.BlockSpec((None, 8, D_MODEL), lambda i, *_: ((i * tm) // rows_per_group, 0, 0))
    g_spec = pl.BlockSpec((1, D_MODEL), lambda i, *_: (0, 0))
    return x_spec, mod_spec, g_spec


def _mod_kernel(c_ref, w_ref, b_ref, o_ref):
    s = _silu(c_ref[...])
    o_ref[...] = jnp.dot(s, w_ref[...], precision=HIGHEST, preferred_element_type=F32) + b_ref[...]


def _modulation(cond8, ada_w, ada_b):
    depth, _, n = ada_w.shape
    tn = n // 4
    return pl.pallas_call(
        _mod_kernel,
        grid=(depth, n // tn),
        in_specs=[pl.BlockSpec((8, D_MODEL), lambda l, j: (0, 0)),
                  pl.BlockSpec((None, D_MODEL, tn), lambda l, j: (l, 0, j)),
                  pl.BlockSpec((None, 1, tn), lambda l, j: (l, 0, j))],
        out_specs=pl.BlockSpec((None, 8, tn), lambda l, j: (l, 0, j)),
        out_shape=jax.ShapeDtypeStruct((depth, 8, n), F32),
        name="modulation",
    )(cond8, ada_w, ada_b.reshape(depth, 1, n))


def _pre_linear_kernel(x_ref, mod_ref, g_ref, w_ref, *rest, rope_tiles):
    if rope_tiles:
        cos_ref, sin_up_ref, sin_dn_ref, o_ref, h_ref = rest
    else:
        o_ref, h_ref = rest
    j = pl.program_id(1)

    @pl.when(j == 0)
    def _():
        h_ref[...] = _pre(x_ref[...], mod_ref[...], g_ref[...], 0).astype(BF16)

    y = jnp.dot(h_ref[...], w_ref[...], preferred_element_type=F32)
    if not rope_tiles:
        o_ref[...] = y
        return

    @pl.when(j < rope_tiles)
    def _():
        cos, s_up, s_dn = cos_ref[...], sin_up_ref[...], sin_dn_ref[...]
        for s in range(y.shape[1] // LANES):
            ys = y[:, s * LANES:(s + 1) * LANES]
            o_ref[:, s * LANES:(s + 1) * LANES] = (
                ys * cos + pltpu.roll(ys, LANES - 16, 1) * s_up + pltpu.roll(ys, 16, 1) * s_dn)

    @pl.when(j >= rope_tiles)
    def _():
        o_ref[...] = y


def _pre_linear(x, mod, g, w, rows_per_group, rope=None, tn=512):
    t, n = x.shape[0], w.shape[1]
    tm = ROW_TILE
    x_spec, mod_spec, g_spec = _row_specs(tm, rows_per_group)
    in_specs = [x_spec, mod_spec, g_spec, pl.BlockSpec((D_MODEL, tn), lambda i, j: (0, j))]
    args = [x, mod, g, w]
    rope_tiles = 0
    if rope is not None:
        rope_tiles = (2 * D_MODEL) // tn
        in_specs += [pl.BlockSpec((tm, LANES), lambda i, j: (i, 0))] * 3
        args += list(rope)
    return pl.pallas_call(
        functools.partial(_pre_linear_kernel, rope_tiles=rope_tiles),
        grid=(t // tm, n // tn),
        in_specs=in_specs,
        out_specs=pl.BlockSpec((tm, tn), lambda i, j: (i, j)),
        out_shape=jax.ShapeDtypeStruct((t, n), F32),
        scratch_shapes=[pltpu.VMEM((tm, D_MODEL), BF16)],
        name="pre_linear",
    )(*args)


def _linear_post_kernel(a_ref, w_ref, x_ref, mod_ref, g_ref, o_ref):
    y = jnp.dot(a_ref[...].astype(BF16), w_ref[...], preferred_element_type=F32)
    o_ref[...] = _post(x_ref[...], y, mod_ref[...], g_ref[...], 2)


def _linear_post(a, w, x, mod, g, rows_per_group):
    t = x.shape[0]
    tm = ROW_TILE
    x_spec, mod_spec, g_spec = _row_specs(tm, rows_per_group)
    return pl.pallas_call(
        _linear_post_kernel,
        grid=(t // tm,),
        in_specs=[x_spec, pl.BlockSpec((D_MODEL, D_MODEL), lambda i: (0, 0)), x_spec, mod_spec, g_spec],
        out_specs=x_spec,
        out_shape=jax.ShapeDtypeStruct((t, D_MODEL), F32),
        name="linear_post",
    )(a, w, x, mod, g)


def _ffn_kernel(x_ref, mod_ref, g_pre_ref, g_post_ref, wg_ref, wu_ref, wd_ref, o_ref, h_ref, acc_ref):
    f = pl.program_id(1)

    @pl.when(f == 0)
    def _():
        h_ref[...] = _pre(x_ref[...], mod_ref[...], g_pre_ref[...], 3).astype(BF16)
        acc_ref[...] = jnp.zeros_like(acc_ref)

    h = h_ref[...]
    gate = jnp.dot(h, wg_ref[...], preferred_element_type=F32)
    up = jnp.dot(h, wu_ref[...], preferred_element_type=F32)
    acc_ref[...] += jnp.dot((_silu(gate) * up).astype(BF16), wd_ref[...], preferred_element_type=F32)

    @pl.when(f == pl.num_programs(1) - 1)
    def _():
        o_ref[...] = _post(x_ref[...], acc_ref[...], mod_ref[...], g_post_ref[...], 5)


def _ffn_dense(x, mod, g_pre, g_post, wg, wu, wd, rows_per_group, tf):
    t, d_ff = x.shape[0], wg.shape[1]
    tm = ROW_TILE
    x_spec, mod_spec, g_spec = _row_specs(tm, rows_per_group)
    return pl.pallas_call(
        _ffn_kernel,
        grid=(t // tm, d_ff // tf),
        in_specs=[x_spec, mod_spec, g_spec, g_spec,
                  pl.BlockSpec((D_MODEL, tf), lambda i, f: (0, f)),
                  pl.BlockSpec((D_MODEL, tf), lambda i, f: (0, f)),
                  pl.BlockSpec((tf, D_MODEL), lambda i, f: (f, 0))],
        out_specs=x_spec,
        out_shape=jax.ShapeDtypeStruct((t, D_MODEL), F32),
        scratch_shapes=[pltpu.VMEM((tm, D_MODEL), BF16), pltpu.VMEM((tm, D_MODEL), F32)],
        name="ffn_dense",
    )(x, mod, g_pre, g_post, wg, wu, wd)


def _router_kernel(x_ref, mod_ref, g_ref, wr_ref, h_ref, r_ref):
    h = _pre(x_ref[...], mod_ref[...], g_ref[...], 3)
    h_ref[...] = h
    logits = jnp.dot(h, wr_ref[...], precision=HIGHEST, preferred_element_type=F32)
    lane = lax.broadcasted_iota(jnp.int32, logits.shape, 1)
    lg = jnp.where(lane < N_EXPERTS, logits, -jnp.inf)
    m1 = jnp.max(lg, axis=-1, keepdims=True)
    i1 = jnp.min(jnp.where(lg == m1, lane, LANES), axis=-1, keepdims=True)
    lg2 = jnp.where(lane == i1, -jnp.inf, lg)
    m2 = jnp.max(lg2, axis=-1, keepdims=True)
    i2 = jnp.min(jnp.where(lg2 == m2, lane, LANES), axis=-1, keepdims=True)
    e = jnp.exp(m2 - m1)
    w1 = 1.0 / (1.0 + e)
    w2 = e / (1.0 + e)
    r_ref[...] = jnp.where(lane == 0, i1.astype(F32),
                           jnp.where(lane == 1, i2.astype(F32),
                                     jnp.where(lane == 2, w1, jnp.where(lane == 3, w2, 0.0))))


def _router(x, mod, g, w_router, rows_per_group):
    t = x.shape[0]
    tm = ROW_TILE
    x_spec, mod_spec, g_spec = _row_specs(tm, rows_per_group)
    wr = jnp.pad(w_router, ((0, 0), (0, LANES - N_EXPERTS)))
    r_spec = pl.BlockSpec((tm, LANES), lambda i: (i, 0))
    return pl.pallas_call(
        _router_kernel,
        grid=(t // tm,),
        in_specs=[x_spec, mod_spec, g_spec, pl.BlockSpec((D_MODEL, LANES), lambda i: (0, 0))],
        out_specs=[x_spec, r_spec],
        out_shape=[jax.ShapeDtypeStruct((t, D_MODEL), F32), jax.ShapeDtypeStruct((t, LANES), F32)],
        name="moe_router",
    )(x, mod, g, wr)


def _gather_rows_kernel(idx_ref, src_ref, o_ref, sem, *, rows):
    base = pl.program_id(0) * rows

    def row_copy(r, src_row):
        return pltpu.make_async_copy(src_ref.at[pl.ds(src_row, 1), :], o_ref.at[pl.ds(r, 1), :], sem)

    def start(r, carry):
        row_copy(r, idx_ref[base + r]).start()
        return carry

    def wait(r, carry):
        row_copy(r, 0).wait()
        return carry

    lax.fori_loop(0, rows, start, 0)
    lax.fori_loop(0, rows, wait, 0)


def _gather_rows(src, idx, rows=256):
    n = idx.shape[0]
    return pl.pallas_call(
        functools.partial(_gather_rows_kernel, rows=rows),
        grid_spec=pltpu.PrefetchScalarGridSpec(
            num_scalar_prefetch=1,
            grid=(n // rows,),
            in_specs=[pl.BlockSpec(memory_space=pl.ANY)],
            out_specs=pl.BlockSpec((rows, D_MODEL), lambda i, idx_ref: (i, 0)),
            scratch_shapes=[pltpu.SemaphoreType.DMA(())]),
        out_shape=jax.ShapeDtypeStruct((n, D_MODEL), F32),
        name="gather_rows",
    )(idx, src)


def _moe_ffn_kernel(te_ref, nvalid_ref, x_ref, wg_ref, wu_ref, wd_ref, o_ref, xb_ref, acc_ref):
    m, f = pl.program_id(0), pl.program_id(1)
    last = f == pl.num_programs(1) - 1
    valid = m < nvalid_ref[0]

    @pl.when(valid)
    def _():
        @pl.when(f == 0)
        def _():
            xb_ref[...] = x_ref[...].astype(BF16)
            acc_ref[...] = jnp.zeros_like(acc_ref)

        h = xb_ref[...]
        gate = jnp.dot(h, wg_ref[...], preferred_element_type=F32)
        up = jnp.dot(h, wu_ref[...], preferred_element_type=F32)
        acc_ref[...] += jnp.dot((_silu(gate) * up).astype(BF16), wd_ref[...], preferred_element_type=F32)

        @pl.when(last)
        def _():
            o_ref[...] = acc_ref[...]

    @pl.when(jnp.logical_not(valid) & last)
    def _():
        o_ref[...] = jnp.zeros_like(o_ref)


def _moe_ffn(x_sorted, tile_expert, n_valid, wg, wu, wd, tm, tf):
    s, d_ff = x_sorted.shape[0], wg.shape[2]
    nf = d_ff // tf

    def m_eff(m, nv):
        return jnp.minimum(m, nv[0] - 1)

    def f_eff(m, f, nv):
        return jnp.where(m < nv[0], f, nf - 1)

    return pl.pallas_call(
        _moe_ffn_kernel,
        grid_spec=pltpu.PrefetchScalarGridSpec(
            num_scalar_prefetch=2,
            grid=(s // tm, nf),
            in_specs=[pl.BlockSpec((tm, D_MODEL), lambda m, f, te, nv: (m_eff(m, nv), 0)),
                      pl.BlockSpec((None, D_MODEL, tf), lambda m, f, te, nv: (te[m_eff(m, nv)], 0, f_eff(m, f, nv))),
                      pl.BlockSpec((None, D_MODEL, tf), lambda m, f, te, nv: (te[m_eff(m, nv)], 0, f_eff(m, f, nv))),
                      pl.BlockSpec((None, tf, D_MODEL), lambda m, f, te, nv: (te[m_eff(m, nv)], f_eff(m, f, nv), 0))],
            out_specs=pl.BlockSpec((tm, D_MODEL), lambda m, f, te, nv: (m, 0)),
            scratch_shapes=[pltpu.VMEM((tm, D_MODEL), BF16), pltpu.VMEM((tm, D_MODEL), F32)]),
        out_shape=jax.ShapeDtypeStruct((s, D_MODEL), F32),
        name="moe_ffn",
    )(tile_expert, n_valid, x_sorted, wg, wu, wd)


def _moe_post_kernel(y1_ref, y2_ref, r_ref, x_ref, mod_ref, g_ref, o_ref):
    r = r_ref[...]
    y = r[:, 2:3] * y1_ref[...] + r[:, 3:4] * y2_ref[...]
    o_ref[...] = _post(x_ref[...], y, mod_ref[...], g_ref[...], 5)


def _moe_post(y_pair, route, x, mod, g, rows_per_group):
    t = x.shape[0]
    tm = ROW_TILE
    x_spec, mod_spec, g_spec = _row_specs(tm, rows_per_group)
    nb = t // tm
    return pl.pallas_call(
        _moe_post_kernel,
        grid=(nb,),
        in_specs=[x_spec, pl.BlockSpec((tm, D_MODEL), lambda i: (i + nb, 0)),
                  pl.BlockSpec((tm, LANES), lambda i: (i, 0)), x_spec, mod_spec, g_spec],
        out_specs=x_spec,
        out_shape=jax.ShapeDtypeStruct((t, D_MODEL), F32),
        name="moe_post",
    )(y_pair, y_pair, route, x, mod, g)


def _moe_slots(e1, e2, tm):
    t = e1.shape[0]
    e_flat = jnp.concatenate([e1, e2])
    onehot = (e_flat[:, None] == jnp.arange(N_EXPERTS, dtype=jnp.int32)[None, :]).astype(jnp.int32)
    csum = jnp.cumsum(onehot, axis=0)
    rank = jnp.sum((csum - onehot) * onehot, axis=1)
    tiles_e = (csum[-1] + tm - 1) // tm
    tile_end = jnp.cumsum(tiles_e)
    slot = jnp.sum(onehot * ((tile_end - tiles_e) * tm)[None, :], axis=1) + rank
    n_tiles = (2 * t) // tm + N_EXPERTS
    tile_expert = jnp.sum(jnp.arange(n_tiles, dtype=jnp.int32)[:, None] >= tile_end[None, :], axis=1)
    tile_expert = jnp.minimum(tile_expert, N_EXPERTS - 1).astype(jnp.int32)
    n_valid = tile_end[-1:].astype(jnp.int32)
    tok = jnp.concatenate([jnp.arange(t, dtype=jnp.int32)] * 2)
    src = jnp.zeros((n_tiles * tm,), jnp.int32).at[slot].set(tok)
    return slot.astype(jnp.int32), src, tile_expert, n_valid


def _moe(x, mod, g_pre, g_post, w_router, wg, wu, wd, rows_per_group, tm=512, tf=512):
    h, route = _router(x, mod, g_pre, w_router, rows_per_group)
    e1 = route[:, 0].astype(jnp.int32)
    e2 = route[:, 1].astype(jnp.int32)
    slot, src, tile_expert, n_valid = _moe_slots(e1, e2, tm)
    x_sorted = _gather_rows(h, src)
    y_sorted = _moe_ffn(x_sorted, tile_expert, n_valid, wg, wu, wd, tm, tf)
    y_pair = _gather_rows(y_sorted, slot)
    return _moe_post(y_pair, route, x, mod, g_post, rows_per_group)


def _s5_matrices(a_re, a_im, b_re, b_im, c_re, c_im, log_step):
    g_, p_, j_, lc = S5_GROUPS, S5_STATE, S5_GROUP, S5_CHUNK
    lam = lax.complex(a_re.astype(F32), a_im.astype(F32))
    z = lam * jnp.exp(log_step.astype(F32))[..., None]
    k = jnp.arange(lc + 1, dtype=F32)
    lam_pow = jnp.exp(z[None] * k[:, None, None, None])
    b_bar = ((lam_pow[1] - 1.0) / lam)[..., None] * lax.complex(b_re.astype(F32), b_im.astype(F32))
    c = lax.complex(c_re.astype(F32), c_im.astype(F32))

    cl = c[None] * lam_pow[:lc, :, :, None, :]
    kern = (jnp.einsum('kdgjp,dgpi->kdgji', cl.real, b_bar.real, precision=HIGHEST)
            - jnp.einsum('kdgjp,dgpi->kdgji', cl.imag, b_bar.imag, precision=HIGHEST))
    dif = np.arange(lc)[None, :] - np.arange(lc)[:, None]
    kf = kern[:, 0][np.clip(dif, 0, lc - 1)] * jnp.asarray(dif >= 0, F32)[:, :, None, None, None]
    kb = kern[:, 1][np.clip(-dif, 0, lc - 1)] * jnp.asarray(dif <= 0, F32)[:, :, None, None, None]
    kt = jnp.transpose(kf + kb, (2, 0, 4, 1, 3)).reshape(g_, lc * j_, lc * j_)
    kt = kt.reshape(S5_PAIRS, 2, lc * j_, lc * j_)

    ms2, my2 = [], []
    for d in range(2):
        e_s = (lc - 1 - np.arange(lc)) if d == 0 else np.arange(lc)
        f_t = (np.arange(lc) + 1) if d == 0 else (lc - np.arange(lc))
        ms = lam_pow[e_s, d][:, :, :, None] * b_bar[d][None]
        ms = jnp.transpose(ms, (1, 0, 3, 2)).reshape(g_, lc * j_, p_)
        my = c[d][None] * lam_pow[f_t, d][:, :, None, :]
        my = jnp.transpose(my, (1, 3, 0, 2)).reshape(g_, p_, lc * j_)
        ms_pair = jnp.zeros((S5_PAIRS, 2 * lc * j_, 4 * p_), F32)
        my_pair = jnp.zeros((S5_PAIRS, 4 * p_, 2 * lc * j_), F32)
        for gs in range(2):
            rows = slice(gs * lc * j_, (gs + 1) * lc * j_)
            ms_pair = ms_pair.at[:, rows, gs * p_:(gs + 1) * p_].set(ms.real[gs::2])
            ms_pair = ms_pair.at[:, rows, (2 + gs) * p_:(3 + gs) * p_].set(ms.imag[gs::2])
            my_pair = my_pair.at[:, gs * p_:(gs + 1) * p_, rows].set(my.real[gs::2])
            my_pair = my_pair.at[:, (2 + gs) * p_:(3 + gs) * p_, rows].set(-my.imag[gs::2])
        ms2.append(ms_pair)
        my2.append(my_pair)
    lam_c = lam_pow[lc]
    lam_slab = jnp.concatenate([lam_c.real.reshape(2, S5_PAIRS, 2 * p_),
                                lam_c.imag.reshape(2, S5_PAIRS, 2 * p_)], axis=-1)
    return kt.astype(BF16), jnp.stack(ms2).astype(BF16), jnp.stack(my2).astype(BF16), lam_slab


def _state_to_slab(st_re, st_im):
    b = st_re.shape[0]
    re = jnp.transpose(st_re.astype(F32), (1, 0, 2, 3)).reshape(2, b, S5_PAIRS, 2 * S5_STATE)
    im = jnp.transpose(st_im.astype(F32), (1, 0, 2, 3)).reshape(2, b, S5_PAIRS, 2 * S5_STATE)
    return jnp.concatenate([re, im], axis=-1)


def _slab_to_state(slab):
    b = slab.shape[1]
    re = slab[..., :2 * S5_STATE].reshape(2, b, S5_GROUPS, S5_STATE)
    im = slab[..., 2 * S5_STATE:].reshape(2, b, S5_GROUPS, S5_STATE)
    return jnp.transpose(re, (1, 0, 2, 3)), jnp.transpose(im, (1, 0, 2, 3))


def _s5_pre_kernel(x_ref, mod_ref, g_ref, o_ref):
    o_ref[...] = _pre(x_ref[...], mod_ref[...], g_ref[...], 0).astype(BF16)


def _s5_pre(x, mod, g, rows_per_group):
    t = x.shape[0]
    tm = ROW_TILE
    x_spec, mod_spec, g_spec = _row_specs(tm, rows_per_group)
    return pl.pallas_call(
        _s5_pre_kernel, grid=(t // tm,), in_specs=[x_spec, mod_spec, g_spec], out_specs=x_spec,
        out_shape=jax.ShapeDtypeStruct((t, D_MODEL), BF16), name="s5_pre",
    )(x, mod, g)


def _s5_inject_kernel(u_ref, ms_ref, o_ref):
    u = u_ref[...]
    for d in range(2):
        o_ref[d] = jnp.dot(u, ms_ref[d], preferred_element_type=F32)


def _s5_inject(u_pairs, ms2):
    _, r, k = u_pairs.shape
    n = ms2.shape[-1]
    return pl.pallas_call(
        _s5_inject_kernel,
        grid=(S5_PAIRS,),
        in_specs=[pl.BlockSpec((None, r, k), lambda p: (p, 0, 0)),
                  pl.BlockSpec((2, None, k, n), lambda p: (0, p, 0, 0))],
        out_specs=pl.BlockSpec((2, None, r, n), lambda p: (0, p, 0, 0)),
        out_shape=jax.ShapeDtypeStruct((2, S5_PAIRS, r, n), F32),
        name="s5_inject",
    )(u_pairs, ms2)


def _s5_scan_kernel(s_ref, lam_ref, h0_ref, hin_ref, hfin_ref, st_ref, *, tb):
    d, kblk = pl.program_id(0), pl.program_id(2)
    half = 2 * S5_STATE

    @pl.when(kblk == 0)
    def _():
        st_ref[...] = h0_ref[...]

    lam = lam_ref[...]
    lr, li = lam[:, :half], lam[:, half:]

    def body(i, carry):
        hr, hi = carry
        idx = jnp.where(d == 0, i, tb - 1 - i)
        hin_ref[idx, :, :half] = hr
        hin_ref[idx, :, half:] = hi
        s = s_ref[idx]
        return lr * hr - li * hi + s[:, :half], lr * hi + li * hr + s[:, half:]

    st = st_ref[...]
    hr, hi = lax.fori_loop(0, tb, body, (st[:, :half], st[:, half:]))
    st_ref[:, :half] = hr
    st_ref[:, half:] = hi

    @pl.when(kblk == pl.num_programs(2) - 1)
    def _():
        hfin_ref[...] = st_ref[...]


def _s5_scan(s, lam_slab, h0, tb):
    _, n_seq, nc, pairs, w = s.shape
    nk = nc // tb

    def blk(d, b, k):
        return (d, b, k + d * (nk - 1 - 2 * k), 0, 0)

    return pl.pallas_call(
        functools.partial(_s5_scan_kernel, tb=tb),
        grid=(2, n_seq, nk),
        in_specs=[pl.BlockSpec((None, None, tb, pairs, w), blk),
                  pl.BlockSpec((None, pairs, w), lambda d, b, k: (d, 0, 0)),
                  pl.BlockSpec((None, None, pairs, w), lambda d, b, k: (d, b, 0, 0))],
        out_specs=[pl.BlockSpec((None, None, tb, pairs, w), blk),
                   pl.BlockSpec((None, None, pairs, w), lambda d, b, k: (d, b, 0, 0))],
        out_shape=[jax.ShapeDtypeStruct(s.shape, F32), jax.ShapeDtypeStruct((2, n_seq, pairs, w), F32)],
        scratch_shapes=[pltpu.VMEM((pairs, w), F32)],
        name="s5_scan",
    )(s, lam_slab, h0)


def _s5_readout_kernel(u_ref, kt_ref, hin_ref, my_ref, o_ref):
    u = u_ref[...]
    half = u.shape[1] // 2
    yh = (jnp.dot(hin_ref[0].astype(BF16), my_ref[0], preferred_element_type=F32)
          + jnp.dot(hin_ref[1].astype(BF16), my_ref[1], preferred_element_type=F32))
    o_ref[:, :half] = jnp.dot(u[:, :half], kt_ref[0], preferred_element_type=F32) + yh[:, :half]
    o_ref[:, half:] = jnp.dot(u[:, half:], kt_ref[1], preferred_element_type=F32) + yh[:, half:]


def _s5_readout(u_pairs, kt, hin, my2):
    _, r, k = u_pairs.shape
    w = hin.shape[-1]
    return pl.pallas_call(
        _s5_readout_kernel,
        grid=(S5_PAIRS,),
        in_specs=[pl.BlockSpec((None, r, k), lambda p: (p, 0, 0)),
                  pl.BlockSpec((None, 2, k // 2, k // 2), lambda p: (p, 0, 0, 0)),
                  pl.BlockSpec((2, None, r, w), lambda p: (0, p, 0, 0)),
                  pl.BlockSpec((2, None, w, k), lambda p: (0, p, 0, 0))],
        out_specs=pl.BlockSpec((None, r, k), lambda p: (p, 0, 0)),
        out_shape=jax.ShapeDtypeStruct((S5_PAIRS, r, k), F32),
        name="s5_readout",
    )(u_pairs, kt, hin, my2)


def _s5_glu_post_kernel(x_ref, mod_ref, g_pre_ref, g_post_ref, d_ref, y_ref, wa_ref, wb_ref, o_ref):
    x, mod = x_ref[...], mod_ref[...]
    y = d_ref[...] * _pre(x, mod, g_pre_ref[...], 0) + y_ref[...]
    y = (0.5 * y * (1.0 + jnp.tanh(math.sqrt(2.0 / math.pi) * (y + 0.044715 * (y * y * y))))).astype(BF16)
    out = (jnp.dot(y, wa_ref[...], preferred_element_type=F32)
           * jax.nn.sigmoid(jnp.dot(y, wb_ref[...], preferred_element_type=F32)))
    o_ref[...] = _post(x, out, mod, g_post_ref[...], 2)


def _s5_glu_post(x, mod, g_pre, g_post, d_skip, y, wa, wb, rows_per_group):
    t = x.shape[0]
    tm = ROW_TILE
    x_spec, mod_spec, g_spec = _row_specs(tm, rows_per_group)
    w_spec = pl.BlockSpec((D_MODEL, D_MODEL), lambda i: (0, 0))
    return pl.pallas_call(
        _s5_glu_post_kernel, grid=(t // tm,),
        in_specs=[x_spec, mod_spec, g_spec, g_spec, g_spec, x_spec, w_spec, w_spec],
        out_specs=x_spec, out_shape=jax.ShapeDtypeStruct((t, D_MODEL), F32), name="s5_glu_post",
    )(x, mod, g_pre, g_post, d_skip, y, wa, wb)


def _s5_layer(x, mod, g_pre, g_post, params, st_re, st_im, n_ctx_seq, ctx_len, n_lat_seq, lat_len,
              rows_per_group):
    a_re, a_im, b_re, b_im, c_re, c_im, log_step, d_skip, w_glu_a, w_glu_b = params
    t = x.shape[0]
    lc, gj = S5_CHUNK, 2 * S5_GROUP
    r = t // lc
    kt, ms2, my2, lam_slab = _s5_matrices(a_re, a_im, b_re, b_im, c_re, c_im, log_step)

    u = _s5_pre(x, mod, g_pre, rows_per_group)
    u_pairs = jnp.transpose(u.reshape(r, lc, S5_PAIRS, 2, S5_GROUP), (2, 0, 3, 1, 4)).reshape(S5_PAIRS, r, 2 * lc * S5_GROUP)
    inj = jnp.transpose(_s5_inject(u_pairs, ms2), (0, 2, 1, 3))

    r_ctx = n_ctx_seq * ctx_len // lc
    w = inj.shape[-1]
    s_ctx = inj[:, :r_ctx].reshape(2, n_ctx_seq, ctx_len // lc, S5_PAIRS, w)
    s_lat = inj[:, r_ctx:].reshape(2, n_lat_seq, lat_len // lc, S5_PAIRS, w)
    hin_ctx, hfin_ctx = _s5_scan(s_ctx, lam_slab, jnp.zeros((2, n_ctx_seq, S5_PAIRS, w), F32), tb=ctx_len // lc)
    hin_lat, _ = _s5_scan(s_lat, lam_slab, _state_to_slab(st_re, st_im), tb=min(64, lat_len // lc))
    hin = jnp.concatenate([hin_ctx.reshape(2, r_ctx, S5_PAIRS, w), hin_lat.reshape(2, r - r_ctx, S5_PAIRS, w)], axis=1)
    hin = jnp.transpose(hin, (0, 2, 1, 3))

    y_pairs = _s5_readout(u_pairs, kt, hin, my2)
    y = jnp.transpose(y_pairs.reshape(S5_PAIRS, r, 2, lc, S5_GROUP), (1, 3, 0, 2, 4)).reshape(t, D_MODEL)
    x_new = _s5_glu_post(x, mod, g_pre, g_post, d_skip.reshape(1, D_MODEL).astype(F32), y,
                         w_glu_a.astype(BF16), w_glu_b.astype(BF16), rows_per_group)
    new_re, new_im = _slab_to_state(hfin_ctx)
    return x_new, new_re, new_im


def _split_maps(q):
    lane = lax.broadcasted_iota(jnp.int32, q.shape, 1)
    low = lane < HEAD_DIM
    q = q * ATTN_SCALE
    return low, jnp.where(low, q, 0.0).astype(BF16), jnp.where(low, 0.0, q).astype(BF16)


def _scores(q, k):
    return lax.dot_general(q, k, (((1,), (1,)), ((), ())), preferred_element_type=F32)


def _attn_kernel(*refs, n1, n2, ck, diff, sub_scale):
    q_ref, k1_ref, v1_ref = refs[:3]
    refs = refs[3:]
    if n2:
        k2_ref, v2_ref = refs[:2]
        refs = refs[2:]
    if diff:
        lam_ref, gsub_ref = refs[:2]
        refs = refs[2:]
    o_ref, = refs
    low, q_lo, q_hi = _split_maps(q_ref[...])
    tq = q_lo.shape[0]

    def step(k, v, carry):
        kb, vb = k.astype(BF16), v.astype(BF16)
        out = []
        for qm, (m_old, l_old, acc) in zip((q_lo, q_hi), carry):
            s = _scores(qm, kb)
            m_new = jnp.maximum(m_old, jnp.max(s, axis=-1, keepdims=True))
            alpha = jnp.exp(m_old - m_new)
            p = jnp.exp(s - m_new)
            l_new = alpha * l_old + jnp.sum(p, axis=-1, keepdims=True)
            acc = alpha * acc + jnp.dot(p.astype(BF16), vb, preferred_element_type=F32)
            out.append((m_new, l_new, acc))
        return tuple(out)

    init = (jnp.full((tq, 1), -jnp.inf, F32), jnp.zeros((tq, 1), F32), jnp.zeros((tq, LANES), F32))
    carry = (init, init)
    if n1 == 1:
        carry = step(k1_ref[...], v1_ref[...], carry)
    else:
        carry = lax.fori_loop(
            0, n1, lambda i, c: step(k1_ref[pl.ds(i * ck, ck), :], v1_ref[pl.ds(i * ck, ck), :], c), carry)
    for i in range(n2):
        carry = step(k2_ref[i * ck:(i + 1) * ck, :], v2_ref[i * ck:(i + 1) * ck, :], carry)
    (_, l0, a0), (_, l1, a1) = carry
    o0, o1 = a0 / l0, a1 / l1
    if diff:
        o = o0 - lam_ref[...] * o1
        o_ref[...] = _rms(o, gsub_ref[...]) * sub_scale
    else:
        o_ref[...] = jnp.where(low, o0, o1)


def _attention(qkv, n_seq, seq_len, row0, tq, ck, cache=None, diff=None):
    nblk = D_MODEL // LANES
    qb = seq_len // tq
    assert row0 % seq_len == 0 and seq_len % tq == 0 and seq_len % ck == 0
    seq0 = row0 // seq_len
    in_specs = [pl.BlockSpec((tq, LANES), lambda b, h, i: ((seq0 + b) * qb + i, h)),
                pl.BlockSpec((seq_len, LANES), lambda b, h, i: (seq0 + b, nblk + h)),
                pl.BlockSpec((seq_len, LANES), lambda b, h, i: (seq0 + b, 2 * nblk + h))]
    args = [qkv, qkv, qkv]
    n2 = 0
    if cache is not None:
        k_c, v_c = cache
        past = k_c.shape[1]
        assert past % ck == 0
        n2 = past // ck
        in_specs += [pl.BlockSpec((None, past, LANES), lambda b, h, i: (b, 0, h))] * 2
        args += [k_c, v_c]
    sub_scale = None
    if diff is not None:
        lam_row, g_sub, sub_scale = diff
        in_specs += [pl.BlockSpec((1, LANES), lambda b, h, i: (0, 0))] * 2
        args += [lam_row, g_sub]
    return pl.pallas_call(
        functools.partial(_attn_kernel, n1=seq_len // ck, n2=n2, ck=ck, diff=diff is not None, sub_scale=sub_scale),
        grid=(n_seq, nblk, qb),
        in_specs=in_specs,
        out_specs=pl.BlockSpec((tq, LANES), lambda b, h, i: (b * qb + i, h)),
        out_shape=jax.ShapeDtypeStruct((n_seq * seq_len, D_MODEL), F32),
        name="attention",
    )(*args)


def _na_bias_table(rel_bias, rows):
    w = GRID_W
    cc = np.arange(w)
    col_start = np.clip(cc - NA_WIN_C // 2, 0, w - NA_WIN_C)
    col_ok = (cc[None, :] >= col_start[:, None]) & (cc[None, :] < col_start[:, None] + NA_WIN_C)
    d_col = np.clip(cc[None, :] - cc[:, None], 1 - NA_WIN_C, NA_WIN_C - 1) + NA_WIN_C - 1
    n_dr = 2 * NA_WIN_R - 1
    tab = jnp.where(jnp.asarray(col_ok)[None, None], rel_bias.astype(F32)[:, :, d_col], NEG_INF)
    tab = jnp.concatenate([tab, jnp.full_like(tab[:, :1], NEG_INF)], axis=1)
    dr_idx = np.full((3, NA_QROWS, NA_UROWS), n_dr, np.int32)
    nblocks = rows // NA_QROWS
    for var, rb in enumerate((0, 1, nblocks - 1)):
        us = int(np.clip(NA_QROWS * rb - NA_WIN_R // 2, 0, rows - NA_UROWS))
        for a in range(NA_QROWS):
            r = NA_QROWS * rb + a
            ws = int(np.clip(r - NA_WIN_R // 2, 0, rows - NA_WIN_R))
            for kr in range(NA_UROWS):
                if ws <= us + kr < ws + NA_WIN_R:
                    dr_idx[var, a, kr] = us + kr - r + NA_WIN_R - 1
    big = tab[:, dr_idx]
    return jnp.transpose(big, (0, 1, 2, 4, 3, 5)).reshape(rel_bias.shape[0], 3, NA_QROWS * w, NA_UROWS * w)


def _na_latent_kernel(q_ref, k_ref, v_ref, kc_ref, vc_ref, bias_ref, o_ref, kb_ref, vb_ref, *, rows):
    tq, tu = NA_QROWS * GRID_W, NA_UROWS * GRID_W
    nblocks = rows // NA_QROWS
    kb_ref[...] = k_ref[...].astype(BF16)
    vb_ref[...] = v_ref[...].astype(BF16)
    kc, vc = kc_ref[...].astype(BF16), vc_ref[...].astype(BF16)

    def block(rb, carry):
        us = jnp.clip(NA_QROWS * rb - NA_WIN_R // 2, 0, rows - NA_UROWS)
        var = jnp.where(rb == 0, 0, jnp.where(rb == nblocks - 1, 2, 1))
        q0 = pl.multiple_of(rb * tq, tq)
        k0 = pl.multiple_of(us * GRID_W, GRID_W)
        low, q_lo, q_hi = _split_maps(q_ref[pl.ds(q0, tq), :])
        ku, vu = kb_ref[pl.ds(k0, tu), :], vb_ref[pl.ds(k0, tu), :]
        outs = []
        for hd, qm in enumerate((q_lo, q_hi)):
            s_loc = _scores(qm, ku) + bias_ref[hd, var]
            s_ctx = _scores(qm, kc)
            m = jnp.maximum(jnp.max(s_loc, axis=-1, keepdims=True), jnp.max(s_ctx, axis=-1, keepdims=True))
            p_loc, p_ctx = jnp.exp(s_loc - m), jnp.exp(s_ctx - m)
            l = jnp.sum(p_loc, axis=-1, keepdims=True) + jnp.sum(p_ctx, axis=-1, keepdims=True)
            acc = (jnp.dot(p_loc.astype(BF16), vu, preferred_element_type=F32)
                   + jnp.dot(p_ctx.astype(BF16), vc, preferred_element_type=F32))
            outs.append(acc / l)
        o_ref[pl.ds(q0, tq), :] = jnp.where(low, outs[0], outs[1])
        return carry

    lax.fori_loop(0, nblocks, block, 0)


def _na_latent(qkv, row0, n_seq, seq_len, k_ctx, v_ctx, bias_tab):
    nblk = D_MODEL // LANES
    rows = seq_len // GRID_W
    assert row0 % seq_len == 0
    seq0 = row0 // seq_len
    past = k_ctx.shape[1]
    tq, tu = NA_QROWS * GRID_W, NA_UROWS * GRID_W
    col = lambda off: pl.BlockSpec((seq_len, LANES), lambda b, h: (seq0 + b, off + h))
    return pl.pallas_call(
        functools.partial(_na_latent_kernel, rows=rows),
        grid=(n_seq, nblk),
        in_specs=[col(0), col(nblk), col(2 * nblk),
                  pl.BlockSpec((None, past, LANES), lambda b, h: (b, 0, h)),
                  pl.BlockSpec((None, past, LANES), lambda b, h: (b, 0, h)),
                  pl.BlockSpec((2, 3, tq, tu), lambda b, h: (h, 0, 0, 0))],
        out_specs=pl.BlockSpec((seq_len, LANES), lambda b, h: (b, h)),
        out_shape=jax.ShapeDtypeStruct((n_seq * seq_len, D_MODEL), F32),
        scratch_shapes=[pltpu.VMEM((seq_len, LANES), BF16), pltpu.VMEM((seq_len, LANES), BF16)],
        name="na_latent",
    )(qkv, qkv, qkv, k_ctx, v_ctx, bias_tab)


def _rope_tables(t_ctx, lat_len, n_lat_seq):
    pos = np.arange(lat_len)
    n_freq = HEAD_DIM // 4
    inv = jnp.power(ROPE_BASE, -jnp.arange(n_freq, dtype=F32) / n_freq)
    ar = jnp.asarray(pos // GRID_W, F32)[:, None] * inv
    ac = jnp.asarray(pos % GRID_W, F32)[:, None] * inv
    cos = jnp.concatenate([jnp.cos(ar), jnp.cos(ar), jnp.cos(ac), jnp.cos(ac)], axis=-1)
    sin = jnp.concatenate([jnp.sin(ar), jnp.sin(ar), jnp.sin(ac), jnp.sin(ac)], axis=-1)
    first = (np.arange(HEAD_DIM) % (2 * n_freq)) < n_freq
    sin_up = jnp.where(jnp.asarray(first)[None, :], -sin, 0.0)
    sin_dn = jnp.where(jnp.asarray(first)[None, :], 0.0, sin)

    def full(tab, ctx_value):
        tab = jnp.tile(tab, (n_lat_seq, LANES // HEAD_DIM))
        return jnp.concatenate([jnp.full((t_ctx, LANES), ctx_value, F32), tab], axis=0)

    return full(cos, 1.0), full(sin_up, 0.0), full(sin_dn, 0.0)


def kernel(x_prompt, x_sample, state_l0_re, state_l0_im, cache_l1_k, cache_l1_v, cache_l2_k, cache_l2_v, state_l3_re, state_l3_im, c, c_ctx, ada_w, ada_b, norm_g, s5_a_re, s5_a_im, s5_b_re, s5_b_im, s5_c_re, s5_c_im, s5_log_step, s5_d, s5_glu_a, s5_glu_b, na_wqkv, na_wo, na_rel_bias, da_wqkv, da_wo, da_lq1, da_lk1, da_lq2, da_lk2, da_subln_g, ff_wg, ff_wu, ff_wd, moe_router, moe_wg, moe_wu, moe_wd):
    n_ctx_seq, ctx_len, d = x_prompt.shape
    n_lat_seq, lat_len, _ = x_sample.shape
    past = cache_l1_k.shape[1]
    depth = ada_w.shape[0]
    t_ctx = n_ctx_seq * ctx_len
    rows_per_group = lat_len
    assert d == D_MODEL and t_ctx == rows_per_group and n_lat_seq == 2
    caches = [(state_l0_re, state_l0_im), (cache_l1_k, cache_l1_v),
              (cache_l2_k, cache_l2_v), (state_l3_re, state_l3_im)]

    x = jnp.concatenate([x_prompt.reshape(t_ctx, d), x_sample.reshape(n_lat_seq * lat_len, d)], axis=0)
    cond8 = jnp.concatenate([c_ctx[None, :], c, jnp.zeros((8 - 1 - n_lat_seq, d), F32)], axis=0)
    m = _modulation(cond8, ada_w, ada_b)
    mods = jnp.pad(m[:, :1 + n_lat_seq].reshape(depth, 1 + n_lat_seq, 6, d), ((0, 0), (0, 0), (0, 2), (0, 0)))
    rope = _rope_tables(t_ctx, lat_len, n_lat_seq)

    new_state = []
    for i in range(depth):
        kind, j = i % 3, i // 3
        mod = mods[i]
        g = norm_g[i].astype(F32).reshape(4, 1, d)
        if kind == 0:
            params = (s5_a_re[j], s5_a_im[j], s5_b_re[j], s5_b_im[j], s5_c_re[j], s5_c_im[j],
                      s5_log_step[j], s5_d[j], s5_glu_a[j], s5_glu_b[j])
            x, st_re, st_im = _s5_layer(x, mod, g[0], g[1], params, caches[i][0], caches[i][1],
                                        n_ctx_seq, ctx_len, n_lat_seq, lat_len, rows_per_group)
            new_state += [st_re, st_im]
        elif kind == 1:
            qkv = _pre_linear(x, mod, g[0], na_wqkv[j].astype(BF16), rows_per_group)
            o_ctx = _attention(qkv, n_ctx_seq, ctx_len, 0, tq=ctx_len, ck=ctx_len)
            bias_tab = _na_bias_table(na_rel_bias[j], lat_len // GRID_W)
            o_lat = _na_latent(qkv, t_ctx, n_lat_seq, lat_len, caches[i][0].reshape(n_lat_seq, past, d),
                               caches[i][1].reshape(n_lat_seq, past, d), bias_tab)
            x = _linear_post(jnp.concatenate([o_ctx, o_lat], axis=0), na_wo[j].astype(BF16), x, mod, g[1],
                             rows_per_group)
            new_state += [qkv[:t_ctx, d:2 * d].reshape(n_ctx_seq, ctx_len, NA_HEADS, HEAD_DIM),
                          qkv[:t_ctx, 2 * d:].reshape(n_ctx_seq, ctx_len, NA_HEADS, HEAD_DIM)]
        else:
            lam_init = 0.8 - 0.6 * math.exp(-0.3 * i)
            lam = (jnp.exp(jnp.sum(da_lq1[j].astype(F32) * da_lk1[j].astype(F32)))
                   - jnp.exp(jnp.sum(da_lq2[j].astype(F32) * da_lk2[j].astype(F32))) + lam_init)
            diff = (jnp.full((1, LANES), lam, F32), da_subln_g[j].astype(F32).reshape(1, LANES), 1.0 - lam_init)
            qkv = _pre_linear(x, mod, g[0], da_wqkv[j].astype(BF16), rows_per_group, rope=rope)
            o_ctx = _attention(qkv, n_ctx_seq, ctx_len, 0, tq=ctx_len, ck=ctx_len, diff=diff)
            o_lat = _attention(qkv, n_lat_seq, lat_len, t_ctx, tq=256, ck=512,
                               cache=(caches[i][0].reshape(n_lat_seq, past, d), caches[i][1].reshape(n_lat_seq, past, d)),
                               diff=diff)
            x = _linear_post(jnp.concatenate([o_ctx, o_lat], axis=0), da_wo[j].astype(BF16), x, mod, g[1],
                             rows_per_group)
            new_state += [qkv[:t_ctx, d:2 * d].reshape(n_ctx_seq, ctx_len, DA_HEADS, 2, HEAD_DIM),
                          qkv[:t_ctx, 2 * d:].reshape(n_ctx_seq, ctx_len, DA_HEADS, 2 * HEAD_DIM)]
        jf = i // 2
        if i % 2 == 0:
            x = _ffn_dense(x, mod, g[2], g[3], ff_wg[jf].astype(BF16), ff_wu[jf].astype(BF16),
                           ff_wd[jf].astype(BF16), rows_per_group, tf=ff_wg.shape[2] // 2)
        else:
            x = _moe(x, mod, g[2], g[3], moe_router[jf].astype(F32), moe_wg[jf].astype(BF16),
                     moe_wu[jf].astype(BF16), moe_wd[jf].astype(BF16), rows_per_group)
    y_prompt = x[:t_ctx].reshape(n_ctx_seq, ctx_len, d)
    y_sample = x[t_ctx:].reshape(n_lat_seq, lat_len, d)
    return (y_prompt, y_sample, *new_state)
```

```python
import functools
import math

import numpy as np
import jax
import jax.numpy as jnp
from jax import lax
from jax.experimental import pallas as pl
from jax.experimental.pallas import tpu as pltpu

F32 = jnp.float32
BF16 = jnp.bfloat16
HIGHEST = lax.Precision.HIGHEST

D_MODEL = 1024
GRID_W = 64
S5_GROUP = 16
S5_GROUPS = D_MODEL // S5_GROUP
S5_STATE = 64
NA_HEADS = 16
NA_WIN_R = 8
NA_WIN_C = 16
DA_HEADS = 8
HEAD_DIM = 64
ROPE_BASE = 10000.0
N_EXPERTS = 8
NORM_EPS = 1e-6
NEG_INF = -1e30
LANES = 128
SUBLANES = 8
ATTN_SCALE = HEAD_DIM ** -0.5

ROW_TILE = 512
NA_QROWS = 4
NA_UROWS = NA_QROWS + NA_WIN_R

ROW_CHUNKS = D_MODEL // LANES
S5_CHUNK = SUBLANES
S5_BLOCKS = D_MODEL // LANES
S5_BLOCK_GROUPS = LANES // S5_GROUP
S5_BSTATE = S5_BLOCK_GROUPS * S5_STATE


def _rms(x, g):
    return x * lax.rsqrt(jnp.mean(x * x, axis=-1, keepdims=True) + NORM_EPS) * g


def _pre(x, mod, g, shift_row):
    return _rms(x, g) * (1.0 + mod[shift_row + 1:shift_row + 2, :]) + mod[shift_row:shift_row + 1, :]


def _post(x, y, mod, g, gate_row):
    return x + mod[gate_row:gate_row + 1, :] * _rms(y, g)


def _silu(x):
    return x * jax.nn.sigmoid(x)


def _row_specs(tm, rows_per_group):
    x_spec = pl.BlockSpec((tm, D_MODEL), lambda i, *_: (i, 0))
    mod_spec = pl.BlockSpec((None, 8, D_MODEL), lambda i, *_: ((i * tm) // rows_per_group, 0, 0))
    g_spec = pl.BlockSpec((1, D_MODEL), lambda i, *_: (0, 0))
    return x_spec, mod_spec, g_spec


def _mod_kernel(c_ref, w_ref, b_ref, o_ref):
    s = _silu(c_ref[...])
    o_ref[...] = jnp.dot(s, w_ref[...], precision=HIGHEST, preferred_element_type=F32) + b_ref[...]


def _modulation(cond8, ada_w, ada_b):
    depth, _, n = ada_w.shape
    tn = n // 4
    return pl.pallas_call(
        _mod_kernel,
        grid=(depth, n // tn),
        in_specs=[pl.BlockSpec((8, D_MODEL), lambda l, j: (0, 0)),
                  pl.BlockSpec((None, D_MODEL, tn), lambda l, j: (l, 0, j)),
                  pl.BlockSpec((None, 1, tn), lambda l, j: (l, 0, j))],
        out_specs=pl.BlockSpec((None, 8, tn), lambda l, j: (l, 0, j)),
        out_shape=jax.ShapeDtypeStruct((depth, 8, n), F32),
        name="modulation",
    )(cond8, ada_w, ada_b.reshape(depth, 1, n))


def _pre_linear_kernel(x_ref, mod_ref, g_ref, w_ref, *rest, rope_tiles):
    if rope_tiles:
        cos_ref, sin_up_ref, sin_dn_ref, o_ref, h_ref = rest
    else:
        o_ref, h_ref = rest
    j = pl.program_id(1)

    @pl.when(j == 0)
    def _():
        h_ref[...] = _pre(x_ref[...], mod_ref[...], g_ref[...], 0).astype(BF16)

    y = jnp.dot(h_ref[...], w_ref[...], preferred_element_type=F32)
    if not rope_tiles:
        o_ref[...] = y
        return

    @pl.when(j < rope_tiles)
    def _():
        cos, s_up, s_dn = cos_ref[...], sin_up_ref[...], sin_dn_ref[...]
        for s in range(y.shape[1] // LANES):
            ys = y[:, s * LANES:(s + 1) * LANES]
            o_ref[:, s * LANES:(s + 1) * LANES] = (
                ys * cos + pltpu.roll(ys, LANES - 16, 1) * s_up + pltpu.roll(ys, 16, 1) * s_dn)

    @pl.when(j >= rope_tiles)
    def _():
        o_ref[...] = y


def _pre_linear(x, mod, g, w, rows_per_group, rope=None, tn=512):
    t, n = x.shape[0], w.shape[1]
    tm = ROW_TILE
    x_spec, mod_spec, g_spec = _row_specs(tm, rows_per_group)
    in_specs = [x_spec, mod_spec, g_spec, pl.BlockSpec((D_MODEL, tn), lambda i, j: (0, j))]
    args = [x, mod, g, w]
    rope_tiles = 0
    if rope is not None:
        rope_tiles = (2 * D_MODEL) // tn
        in_specs += [pl.BlockSpec((tm, LANES), lambda i, j: (i, 0))] * 3
        args += list(rope)
    return pl.pallas_call(
        functools.partial(_pre_linear_kernel, rope_tiles=rope_tiles),
        grid=(t // tm, n // tn),
        in_specs=in_specs,
        out_specs=pl.BlockSpec((tm, tn), lambda i, j: (i, j)),
        out_shape=jax.ShapeDtypeStruct((t, n), F32),
        scratch_shapes=[pltpu.VMEM((tm, D_MODEL), BF16)],
        name="pre_linear",
    )(*args)


def _linear_post_kernel(a_ref, w_ref, x_ref, mod_ref, g_ref, o_ref):
    y = jnp.dot(a_ref[...].astype(BF16), w_ref[...], preferred_element_type=F32)
    o_ref[...] = _post(x_ref[...], y, mod_ref[...], g_ref[...], 2)


def _linear_post(a, w, x, mod, g, rows_per_group):
    t = x.shape[0]
    tm = ROW_TILE
    x_spec, mod_spec, g_spec = _row_specs(tm, rows_per_group)
    return pl.pallas_call(
        _linear_post_kernel,
        grid=(t // tm,),
        in_specs=[x_spec, pl.BlockSpec((D_MODEL, D_MODEL), lambda i: (0, 0)), x_spec, mod_spec, g_spec],
        out_specs=x_spec,
        out_shape=jax.ShapeDtypeStruct((t, D_MODEL), F32),
        name="linear_post",
    )(a, w, x, mod, g)


def _ffn_kernel(x_ref, mod_ref, g_pre_ref, g_post_ref, wg_ref, wu_ref, wd_ref, o_ref, h_ref, acc_ref):
    f = pl.program_id(1)

    @pl.when(f == 0)
    def _():
        h_ref[...] = _pre(x_ref[...], mod_ref[...], g_pre_ref[...], 3).astype(BF16)
        acc_ref[...] = jnp.zeros_like(acc_ref)

    h = h_ref[...]
    gate = jnp.dot(h, wg_ref[...], preferred_element_type=F32)
    up = jnp.dot(h, wu_ref[...], preferred_element_type=F32)
    acc_ref[...] += jnp.dot((_silu(gate) * up).astype(BF16), wd_ref[...], preferred_element_type=F32)

    @pl.when(f == pl.num_programs(1) - 1)
    def _():
        o_ref[...] = _post(x_ref[...], acc_ref[...], mod_ref[...], g_post_ref[...], 5)


def _ffn_dense(x, mod, g_pre, g_post, wg, wu, wd, rows_per_group, tf):
    t, d_ff = x.shape[0], wg.shape[1]
    tm = ROW_TILE
    x_spec, mod_spec, g_spec = _row_specs(tm, rows_per_group)
    return pl.pallas_call(
        _ffn_kernel,
        grid=(t // tm, d_ff // tf),
        in_specs=[x_spec, mod_spec, g_spec, g_spec,
                  pl.BlockSpec((D_MODEL, tf), lambda i, f: (0, f)),
                  pl.BlockSpec((D_MODEL, tf), lambda i, f: (0, f)),
                  pl.BlockSpec((tf, D_MODEL), lambda i, f: (f, 0))],
        out_specs=x_spec,
        out_shape=jax.ShapeDtypeStruct((t, D_MODEL), F32),
        scratch_shapes=[pltpu.VMEM((tm, D_MODEL), BF16), pltpu.VMEM((tm, D_MODEL), F32)],
        name="ffn_dense",
    )(x, mod, g_pre, g_post, wg, wu, wd)


def _store_row_tiles(ref, value):
    rows = value.shape[0]
    for s in range(ROW_CHUNKS):
        ref[pl.ds(s, rows, stride=ROW_CHUNKS), :] = value[:, s * LANES:(s + 1) * LANES]


def _load_row_tiles(ref, first_row, rows):
    base = first_row * ROW_CHUNKS
    return jnp.concatenate([ref[pl.ds(base + s, rows, stride=ROW_CHUNKS), :] for s in range(ROW_CHUNKS)], axis=1)


def _row_copy(src_hbm, dst_ref, sem, src_row, dst_row):
    return pltpu.make_async_copy(
        src_hbm.at[pl.ds(pl.multiple_of(src_row * ROW_CHUNKS, ROW_CHUNKS), ROW_CHUNKS), :],
        dst_ref.at[pl.ds(pl.multiple_of(dst_row * ROW_CHUNKS, ROW_CHUNKS), ROW_CHUNKS), :], sem)


def _start_rows(src_hbm, idx_ref, idx_base, dst_ref, dst_base, sem, n):
    def body(r, carry):
        _row_copy(src_hbm, dst_ref, sem, idx_ref[idx_base + r], dst_base + r).start()
        return carry
    lax.fori_loop(0, n, body, 0, unroll=8)


def _wait_rows(src_hbm, dst_ref, dst_base, sem, n):
    def body(r, carry):
        _row_copy(src_hbm, dst_ref, sem, 0, dst_base + r).wait()
        return carry
    lax.fori_loop(0, n, body, 0, unroll=8)


def _router_kernel(x_ref, mod_ref, g_ref, wr_ref, h_ref, r_ref):
    h = _pre(x_ref[...], mod_ref[...], g_ref[...], 3)
    _store_row_tiles(h_ref, h)
    logits = jnp.dot(h, wr_ref[...], precision=HIGHEST, preferred_element_type=F32)
    lane = lax.broadcasted_iota(jnp.int32, logits.shape, 1)
    lg = jnp.where(lane < N_EXPERTS, logits, -jnp.inf)
    m1 = jnp.max(lg, axis=-1, keepdims=True)
    i1 = jnp.min(jnp.where(lg == m1, lane, LANES), axis=-1, keepdims=True)
    lg2 = jnp.where(lane == i1, -jnp.inf, lg)
    m2 = jnp.max(lg2, axis=-1, keepdims=True)
    i2 = jnp.min(jnp.where(lg2 == m2, lane, LANES), axis=-1, keepdims=True)
    e = jnp.exp(m2 - m1)
    w1 = 1.0 / (1.0 + e)
    w2 = e / (1.0 + e)
    r_ref[...] = jnp.where(lane == 0, i1.astype(F32),
                           jnp.where(lane == 1, i2.astype(F32),
                                     jnp.where(lane == 2, w1, jnp.where(lane == 3, w2, 0.0))))


def _router(x, mod, g, w_router, rows_per_group):
    t = x.shape[0]
    tm = ROW_TILE
    x_spec, mod_spec, g_spec = _row_specs(tm, rows_per_group)
    wr = jnp.pad(w_router, ((0, 0), (0, LANES - N_EXPERTS)))
    return pl.pallas_call(
        _router_kernel,
        grid=(t // tm,),
        in_specs=[x_spec, mod_spec, g_spec, pl.BlockSpec((D_MODEL, LANES), lambda i: (0, 0))],
        out_specs=[pl.BlockSpec((tm * ROW_CHUNKS, LANES), lambda i: (i, 0)),
                   pl.BlockSpec((tm, LANES), lambda i: (i, 0))],
        out_shape=[jax.ShapeDtypeStruct((t * ROW_CHUNKS, LANES), F32), jax.ShapeDtypeStruct((t, LANES), F32)],
        name="moe_router",
    )(x, mod, g, wr)


def _moe_ffn_kernel(te_ref, nvalid_ref, src_ref, h_hbm, wg_ref, wu_ref, wd_ref, o_ref,
                    xg_ref, xb_ref, acc_ref, sem, *, tm):
    m, f = pl.program_id(0), pl.program_id(1)
    n_valid = nvalid_ref[0]
    last = f == pl.num_programs(1) - 1
    valid = m < n_valid

    @pl.when(valid & (f == 0))
    def _():
        slot = m % 2

        @pl.when(m == 0)
        def _():
            _start_rows(h_hbm, src_ref, 0, xg_ref, 0, sem.at[0], tm)

        _wait_rows(h_hbm, xg_ref, slot * tm, sem.at[slot], tm)

        @pl.when(m + 1 < n_valid)
        def _():
            _start_rows(h_hbm, src_ref, (m + 1) * tm, xg_ref, (1 - slot) * tm, sem.at[1 - slot], tm)

        xb_ref[...] = _load_row_tiles(xg_ref, slot * tm, tm).astype(BF16)
        acc_ref[...] = jnp.zeros_like(acc_ref)

    @pl.when(valid)
    def _():
        h = xb_ref[...]
        gate = jnp.dot(h, wg_ref[...], preferred_element_type=F32)
        up = jnp.dot(h, wu_ref[...], preferred_element_type=F32)
        acc_ref[...] += jnp.dot((_silu(gate) * up).astype(BF16), wd_ref[...], preferred_element_type=F32)

        @pl.when(last)
        def _():
            _store_row_tiles(o_ref, acc_ref[...])

    @pl.when(jnp.logical_not(valid) & last)
    def _():
        o_ref[...] = jnp.zeros_like(o_ref)


def _moe_ffn(h_tiles, src, tile_expert, n_valid, wg, wu, wd, tm, tf):
    n_tiles, d_ff = tile_expert.shape[0], wg.shape[2]
    nf = d_ff // tf

    def m_eff(m, nv):
        return jnp.minimum(m, nv[0] - 1)

    def f_eff(m, f, nv):
        return jnp.where(m < nv[0], f, nf - 1)

    return pl.pallas_call(
        functools.partial(_moe_ffn_kernel, tm=tm),
        grid_spec=pltpu.PrefetchScalarGridSpec(
            num_scalar_prefetch=3,
            grid=(n_tiles, nf),
            in_specs=[pl.BlockSpec(memory_space=pl.ANY),
                      pl.BlockSpec((None, D_MODEL, tf), lambda m, f, te, nv, src: (te[m_eff(m, nv)], 0, f_eff(m, f, nv))),
                      pl.BlockSpec((None, D_MODEL, tf), lambda m, f, te, nv, src: (te[m_eff(m, nv)], 0, f_eff(m, f, nv))),
                      pl.BlockSpec((None, tf, D_MODEL), lambda m, f, te, nv, src: (te[m_eff(m, nv)], f_eff(m, f, nv), 0))],
            out_specs=pl.BlockSpec((tm * ROW_CHUNKS, LANES), lambda m, f, te, nv, src: (m, 0)),
            scratch_shapes=[pltpu.VMEM((2 * tm * ROW_CHUNKS, LANES), F32), pltpu.VMEM((tm, D_MODEL), BF16),
                            pltpu.VMEM((tm, D_MODEL), F32), pltpu.SemaphoreType.DMA((2,))]),
        out_shape=jax.ShapeDtypeStruct((n_tiles * tm * ROW_CHUNKS, LANES), F32),
        name="moe_ffn",
    )(tile_expert, n_valid, src, h_tiles, wg, wu, wd)


def _moe_post_kernel(slot_ref, y_hbm, r_ref, x_ref, mod_ref, g_ref, o_ref, yg_ref, sem, *, tm, t):
    i = pl.program_id(0)
    buf = i % 2

    def start(step, b):
        for k in range(2):
            _start_rows(y_hbm, slot_ref, k * t + step * tm, yg_ref, (2 * b + k) * tm, sem.at[b], tm)

    @pl.when(i == 0)
    def _():
        start(0, 0)

    for k in range(2):
        _wait_rows(y_hbm, yg_ref, (2 * buf + k) * tm, sem.at[buf], tm)

    @pl.when(i + 1 < pl.num_programs(0))
    def _():
        start(i + 1, 1 - buf)

    r = r_ref[...]
    y = (r[:, 2:3] * _load_row_tiles(yg_ref, 2 * buf * tm, tm)
         + r[:, 3:4] * _load_row_tiles(yg_ref, (2 * buf + 1) * tm, tm))
    o_ref[...] = _post(x_ref[...], y, mod_ref[...], g_ref[...], 5)


def _moe_post(y_tiles, slot, route, x, mod, g, rows_per_group):
    t = x.shape[0]
    tm = ROW_TILE
    x_spec, mod_spec, g_spec = _row_specs(tm, rows_per_group)
    return pl.pallas_call(
        functools.partial(_moe_post_kernel, tm=tm, t=t),
        grid_spec=pltpu.PrefetchScalarGridSpec(
            num_scalar_prefetch=1,
            grid=(t // tm,),
            in_specs=[pl.BlockSpec(memory_space=pl.ANY), pl.BlockSpec((tm, LANES), lambda i, *_: (i, 0)),
                      x_spec, mod_spec, g_spec],
            out_specs=x_spec,
            scratch_shapes=[pltpu.VMEM((4 * tm * ROW_CHUNKS, LANES), F32), pltpu.SemaphoreType.DMA((2,))]),
        out_shape=jax.ShapeDtypeStruct((t, D_MODEL), F32),
        name="moe_post",
    )(slot, y_tiles, route, x, mod, g)


def _moe_slots(e1, e2, tm):
    t = e1.shape[0]
    e_flat = jnp.concatenate([e1, e2])
    onehot = (e_flat[:, None] == jnp.arange(N_EXPERTS, dtype=jnp.int32)[None, :]).astype(jnp.int32)
    csum = jnp.cumsum(onehot, axis=0)
    rank = jnp.sum((csum - onehot) * onehot, axis=1)
    tiles_e = (csum[-1] + tm - 1) // tm
    tile_end = jnp.cumsum(tiles_e)
    slot = jnp.sum(onehot * ((tile_end - tiles_e) * tm)[None, :], axis=1) + rank
    n_tiles = (2 * t) // tm + N_EXPERTS
    tile_expert = jnp.sum(jnp.arange(n_tiles, dtype=jnp.int32)[:, None] >= tile_end[None, :], axis=1)
    tile_expert = jnp.minimum(tile_expert, N_EXPERTS - 1).astype(jnp.int32)
    n_valid = tile_end[-1:].astype(jnp.int32)
    tok = jnp.concatenate([jnp.arange(t, dtype=jnp.int32)] * 2)
    src = jnp.zeros((n_tiles * tm,), jnp.int32).at[slot].set(tok)
    return slot.astype(jnp.int32), src, tile_expert, n_valid


def _moe(x, mod, g_pre, g_post, w_router, wg, wu, wd, rows_per_group, tm=512, tf=512):
    h_tiles, route = _router(x, mod, g_pre, w_router, rows_per_group)
    e1 = route[:, 0].astype(jnp.int32)
    e2 = route[:, 1].astype(jnp.int32)
    slot, src, tile_expert, n_valid = _moe_slots(e1, e2, tm)
    y_tiles = _moe_ffn(h_tiles, src, tile_expert, n_valid, wg, wu, wd, tm, tf)
    return _moe_post(y_tiles, slot, route, x, mod, g_post, rows_per_group)


def _block_diag(x, group_axis, new_axis):
    out = jnp.expand_dims(x, new_axis)
    shape = [1] * out.ndim
    shape[group_axis if group_axis < new_axis else group_axis + 1] = S5_BLOCK_GROUPS
    shape[new_axis] = S5_BLOCK_GROUPS
    return out * jnp.eye(S5_BLOCK_GROUPS, dtype=F32).reshape(shape)


def _s5_matrices(a_re, a_im, b_re, b_im, c_re, c_im, log_step):
    lc, nb, g8, p_, j_ = S5_CHUNK, S5_BLOCKS, S5_BLOCK_GROUPS, S5_STATE, S5_GROUP
    a_re, a_im = a_re.astype(F32), a_im.astype(F32)
    b_re, b_im = b_re.astype(F32), b_im.astype(F32)
    c_re, c_im = c_re.astype(F32), c_im.astype(F32)
    step = jnp.exp(log_step.astype(F32))[..., None]
    zr, zi = a_re * step, a_im * step
    k = jnp.arange(lc + 1, dtype=F32)[:, None, None, None]
    mag = jnp.exp(k * zr[None])
    pr, pi = mag * jnp.cos(k * zi[None]), mag * jnp.sin(k * zi[None])
    den = a_re * a_re + a_im * a_im
    wr = ((pr[1] - 1.0) * a_re + pi[1] * a_im) / den
    wi = (pi[1] * a_re - (pr[1] - 1.0) * a_im) / den
    bbr = wr[..., None] * b_re - wi[..., None] * b_im
    bbi = wr[..., None] * b_im + wi[..., None] * b_re

    clr = c_re[None] * pr[:lc, :, :, None, :] - c_im[None] * pi[:lc, :, :, None, :]
    cli = c_re[None] * pi[:lc, :, :, None, :] + c_im[None] * pr[:lc, :, :, None, :]
    taps = (jnp.einsum('kdgjp,dgpi->kdgji', clr, bbr, precision=HIGHEST)
            - jnp.einsum('kdgjp,dgpi->kdgji', cli, bbi, precision=HIGHEST))
    dif = np.arange(lc)[None, :] - np.arange(lc)[:, None]
    kf = taps[:, 0][np.clip(dif, 0, lc - 1)] * jnp.asarray(dif >= 0, F32)[:, :, None, None, None]
    kb = taps[:, 1][np.clip(-dif, 0, lc - 1)] * jnp.asarray(dif <= 0, F32)[:, :, None, None, None]
    kt = (kf + kb).reshape(lc, lc, nb, g8, j_, j_)
    kt = jnp.transpose(kt, (2, 0, 3, 5, 1, 4))
    kt = _block_diag(kt, 2, 5).reshape(nb, lc * LANES, lc * LANES)

    ms, my = [], []
    for d in range(2):
        e_s = (lc - 1 - np.arange(lc)) if d == 0 else np.arange(lc)
        f_t = (np.arange(lc) + 1) if d == 0 else (lc - np.arange(lc))
        er, ei = pr[e_s, d][..., None], pi[e_s, d][..., None]
        parts = []
        for part in (er * bbr[d][None] - ei * bbi[d][None], er * bbi[d][None] + ei * bbr[d][None]):
            part = jnp.transpose(part.reshape(lc, nb, g8, p_, j_), (1, 0, 2, 4, 3))
            parts.append(_block_diag(part, 2, 4))
        ms.append(jnp.stack(parts, axis=4).reshape(nb, lc * LANES, 2 * S5_BSTATE))
        fr, fi = pr[f_t, d][:, :, None, :], pi[f_t, d][:, :, None, :]
        for part in (c_re[d][None] * fr - c_im[d][None] * fi, -(c_re[d][None] * fi + c_im[d][None] * fr)):
            part = jnp.transpose(part.reshape(lc, nb, g8, j_, p_), (1, 2, 4, 0, 3))
            my.append(_block_diag(part, 1, 4).reshape(nb, S5_BSTATE, lc * LANES))
    my = jnp.concatenate(my, axis=1)
    lam_slab = jnp.concatenate([pr[lc].reshape(2, nb, S5_BSTATE), pi[lc].reshape(2, nb, S5_BSTATE)], axis=-1)
    return kt.astype(BF16), jnp.stack(ms).astype(BF16), my.astype(BF16), lam_slab


def _state_to_slab(st_re, st_im):
    b = st_re.shape[0]
    re = jnp.transpose(st_re.astype(F32), (1, 0, 2, 3)).reshape(2, b, S5_BLOCKS, S5_BSTATE)
    im = jnp.transpose(st_im.astype(F32), (1, 0, 2, 3)).reshape(2, b, S5_BLOCKS, S5_BSTATE)
    return jnp.concatenate([re, im], axis=-1)


def _slab_to_state(slab):
    b = slab.shape[1]
    re = slab[..., :S5_BSTATE].reshape(2, b, S5_GROUPS, S5_STATE)
    im = slab[..., S5_BSTATE:].reshape(2, b, S5_GROUPS, S5_STATE)
    return jnp.transpose(re, (1, 0, 2, 3)), jnp.transpose(im, (1, 0, 2, 3))


def _s5_pre_kernel(x_ref, mod_ref, g_ref, o_ref):
    o_ref[...] = _pre(x_ref[...], mod_ref[...], g_ref[...], 0)


def _s5_pre(x, mod, g, rows_per_group):
    t = x.shape[0]
    tm = ROW_TILE
    x_spec, mod_spec, g_spec = _row_specs(tm, rows_per_group)
    return pl.pallas_call(
        _s5_pre_kernel, grid=(t // tm,), in_specs=[x_spec, mod_spec, g_spec], out_specs=x_spec,
        out_shape=jax.ShapeDtypeStruct((t, D_MODEL), F32), name="s5_pre",
    )(x, mod, g)


def _chunk_rows(ref, rows):
    return jnp.concatenate([ref[pl.ds(t, rows, stride=S5_CHUNK), :] for t in range(S5_CHUNK)], axis=1).astype(BF16)


def _s5_inject_kernel(u_ref, ms_ref, o_ref):
    u = _chunk_rows(u_ref, o_ref.shape[1])
    for d in range(2):
        o_ref[d] = jnp.dot(u, ms_ref[d], preferred_element_type=F32)


def _s5_inject(u, ms, rows):
    t = u.shape[0]
    r = t // S5_CHUNK
    k, n = ms.shape[-2:]
    return pl.pallas_call(
        _s5_inject_kernel,
        grid=(S5_BLOCKS, r // rows),
        in_specs=[pl.BlockSpec((rows * S5_CHUNK, LANES), lambda cb, rb: (rb, cb)),
                  pl.BlockSpec((2, None, k, n), lambda cb, rb: (0, cb, 0, 0))],
        out_specs=pl.BlockSpec((2, rows, n), lambda cb, rb: (0, rb, cb)),
        out_shape=jax.ShapeDtypeStruct((2, r, S5_BLOCKS * n), F32),
        name="s5_inject",
    )(u, ms)


def _s5_scan_kernel(s_ref, lam_ref, h0_ref, hin_ref, hfin_ref, st_ref, *, tb):
    d, kblk = pl.program_id(0), pl.program_id(2)
    half = S5_BSTATE

    @pl.when(kblk == 0)
    def _():
        st_ref[...] = h0_ref[...]

    lam = lam_ref[...]
    lr, li = lam[:, :half], lam[:, half:]

    def body(i, carry):
        hr, hi = carry
        idx = jnp.where(d == 0, i, tb - 1 - i)
        hin_ref[idx, :, :half] = hr
        hin_ref[idx, :, half:] = hi
        s = s_ref[idx]
        return lr * hr - li * hi + s[:, :half], lr * hi + li * hr + s[:, half:]

    st = st_ref[...]
    hr, hi = lax.fori_loop(0, tb, body, (st[:, :half], st[:, half:]))
    st_ref[:, :half] = hr
    st_ref[:, half:] = hi

    @pl.when(kblk == pl.num_programs(2) - 1)
    def _():
        hfin_ref[...] = st_ref[...]


def _s5_scan(s, lam_slab, h0, tb):
    _, n_seq, nc, nb, w = s.shape
    nk = nc // tb

    def blk(d, b, k):
        return (d, b, k + d * (nk - 1 - 2 * k), 0, 0)

    return pl.pallas_call(
        functools.partial(_s5_scan_kernel, tb=tb),
        grid=(2, n_seq, nk),
        in_specs=[pl.BlockSpec((None, None, tb, nb, w), blk),
                  pl.BlockSpec((None, nb, w), lambda d, b, k: (d, 0, 0)),
                  pl.BlockSpec((None, None, nb, w), lambda d, b, k: (d, b, 0, 0))],
        out_specs=[pl.BlockSpec((None, None, tb, nb, w), blk),
                   pl.BlockSpec((None, None, nb, w), lambda d, b, k: (d, b, 0, 0))],
        out_shape=[jax.ShapeDtypeStruct(s.shape, F32), jax.ShapeDtypeStruct((2, n_seq, nb, w), F32)],
        scratch_shapes=[pltpu.VMEM((nb, w), F32)],
        name="s5_scan",
    )(s, lam_slab, h0)


def _s5_readout_kernel(u_ref, kt_ref, hin_ref, my_ref, o_ref):
    rows = hin_ref.shape[1]
    u = _chunk_rows(u_ref, rows)
    h = jnp.concatenate([hin_ref[0], hin_ref[1]], axis=1).astype(BF16)
    y = (jnp.dot(u, kt_ref[...], preferred_element_type=F32)
         + jnp.dot(h, my_ref[...], preferred_element_type=F32))
    for t in range(S5_CHUNK):
        o_ref[pl.ds(t, rows, stride=S5_CHUNK), :] = y[:, t * LANES:(t + 1) * LANES]


def _s5_readout(u, kt, hin, my, rows):
    t = u.shape[0]
    r = t // S5_CHUNK
    w = hin.shape[-1] // S5_BLOCKS
    k = kt.shape[-1]
    return pl.pallas_call(
        _s5_readout_kernel,
        grid=(S5_BLOCKS, r // rows),
        in_specs=[pl.BlockSpec((rows * S5_CHUNK, LANES), lambda cb, rb: (rb, cb)),
                  pl.BlockSpec((None, k, k), lambda cb, rb: (cb, 0, 0)),
                  pl.BlockSpec((2, rows, w), lambda cb, rb: (0, rb, cb)),
                  pl.BlockSpec((None, 2 * w, k), lambda cb, rb: (cb, 0, 0))],
        out_specs=pl.BlockSpec((rows * S5_CHUNK, LANES), lambda cb, rb: (rb, cb)),
        out_shape=jax.ShapeDtypeStruct((t, D_MODEL), F32),
        name="s5_readout",
    )(u, kt, hin, my)


def _s5_glu_post_kernel(x_ref, mod_ref, g_pre_ref, g_post_ref, d_ref, y_ref, wa_ref, wb_ref, o_ref):
    x, mod = x_ref[...], mod_ref[...]
    y = d_ref[...] * _pre(x, mod, g_pre_ref[...], 0) + y_ref[...]
    y = (0.5 * y * (1.0 + jnp.tanh(math.sqrt(2.0 / math.pi) * (y + 0.044715 * (y * y * y))))).astype(BF16)
    out = (jnp.dot(y, wa_ref[...], preferred_element_type=F32)
           * jax.nn.sigmoid(jnp.dot(y, wb_ref[...], preferred_element_type=F32)))
    o_ref[...] = _post(x, out, mod, g_post_ref[...], 2)


def _s5_glu_post(x, mod, g_pre, g_post, d_skip, y, wa, wb, rows_per_group):
    t = x.shape[0]
    tm = ROW_TILE
    x_spec, mod_spec, g_spec = _row_specs(tm, rows_per_group)
    w_spec = pl.BlockSpec((D_MODEL, D_MODEL), lambda i: (0, 0))
    return pl.pallas_call(
        _s5_glu_post_kernel, grid=(t // tm,),
        in_specs=[x_spec, mod_spec, g_spec, g_spec, g_spec, x_spec, w_spec, w_spec],
        out_specs=x_spec, out_shape=jax.ShapeDtypeStruct((t, D_MODEL), F32), name="s5_glu_post",
    )(x, mod, g_pre, g_post, d_skip, y, wa, wb)


def _s5_layer(x, mod, g_pre, g_post, params, st_re, st_im, n_ctx_seq, ctx_len, n_lat_seq, lat_len,
              rows_per_group):
    a_re, a_im, b_re, b_im, c_re, c_im, log_step, d_skip, w_glu_a, w_glu_b = params
    t = x.shape[0]
    lc = S5_CHUNK
    r = t // lc
    rows = r // 2 if r % 2 == 0 else r
    kt, ms, my, lam_slab = _s5_matrices(a_re, a_im, b_re, b_im, c_re, c_im, log_step)

    u = _s5_pre(x, mod, g_pre, rows_per_group)
    w = 2 * S5_BSTATE
    inj = _s5_inject(u, ms, rows).reshape(2, r, S5_BLOCKS, w)
    r_ctx = n_ctx_seq * ctx_len // lc
    s_ctx = inj[:, :r_ctx].reshape(2, n_ctx_seq, ctx_len // lc, S5_BLOCKS, w)
    s_lat = inj[:, r_ctx:].reshape(2, n_lat_seq, lat_len // lc, S5_BLOCKS, w)
    hin_ctx, hfin_ctx = _s5_scan(s_ctx, lam_slab, jnp.zeros((2, n_ctx_seq, S5_BLOCKS, w), F32), tb=ctx_len // lc)
    hin_lat, _ = _s5_scan(s_lat, lam_slab, _state_to_slab(st_re, st_im), tb=min(64, lat_len // lc))
    hin = jnp.concatenate([hin_ctx.reshape(2, r_ctx, S5_BLOCKS * w), hin_lat.reshape(2, r - r_ctx, S5_BLOCKS * w)],
                          axis=1)
    y = _s5_readout(u, kt, hin, my, rows)
    x_new = _s5_glu_post(x, mod, g_pre, g_post, d_skip.reshape(1, D_MODEL).astype(F32), y,
                         w_glu_a.astype(BF16), w_glu_b.astype(BF16), rows_per_group)
    new_re, new_im = _slab_to_state(hfin_ctx)
    return x_new, new_re, new_im


def _split_maps(q):
    lane = lax.broadcasted_iota(jnp.int32, q.shape, 1)
    low = lane < HEAD_DIM
    q = q * ATTN_SCALE
    return low, jnp.where(low, q, 0.0).astype(BF16), jnp.where(low, 0.0, q).astype(BF16)


def _scores(q, k):
    return lax.dot_general(q, k, (((1,), (1,)), ((), ())), preferred_element_type=F32)


def _attn_kernel(*refs, l1, l2, diff, sub_scale):
    q_ref, k1_ref, v1_ref = refs[:3]
    refs = refs[3:]
    if l2:
        k2_ref, v2_ref = refs[:2]
        refs = refs[2:]
    if diff:
        lam_ref, gsub_ref = refs[:2]
        refs = refs[2:]
    o_ref, kb_ref, vb_ref = refs
    tq = q_ref.shape[0]

    @pl.when(pl.program_id(2) == 0)
    def _():
        kb_ref[0:l1, :] = k1_ref[...].astype(BF16)
        vb_ref[0:l1, 0:LANES] = v1_ref[...].astype(BF16)
        if l2:
            kb_ref[l1:l1 + l2, :] = k2_ref[...].astype(BF16)
            vb_ref[l1:l1 + l2, 0:LANES] = v2_ref[...].astype(BF16)
        vb_ref[:, LANES:] = jnp.ones((l1 + l2, LANES), BF16)

    low, q_lo, q_hi = _split_maps(q_ref[...])
    q2 = jnp.concatenate([q_lo, q_hi], axis=0)
    s = _scores(q2, kb_ref[...])
    p = jnp.exp((s - jnp.max(s, axis=-1, keepdims=True)).astype(BF16))
    acc = jnp.dot(p, vb_ref[...], preferred_element_type=F32)
    o = acc[:, :LANES] / acc[:, LANES:]
    if diff:
        o = o[:tq] - lam_ref[...] * o[tq:]
        o_ref[...] = _rms(o, gsub_ref[...]) * sub_scale
    else:
        o_ref[...] = jnp.where(low, o[:tq], o[tq:])


def _attention(qkv, n_seq, seq_len, row0, tq, cache=None, diff=None):
    nblk = D_MODEL // LANES
    qb = seq_len // tq
    assert row0 % seq_len == 0 and seq_len % tq == 0
    seq0 = row0 // seq_len
    in_specs = [pl.BlockSpec((tq, LANES), lambda b, h, i: ((seq0 + b) * qb + i, h)),
                pl.BlockSpec((seq_len, LANES), lambda b, h, i: (seq0 + b, nblk + h)),
                pl.BlockSpec((seq_len, LANES), lambda b, h, i: (seq0 + b, 2 * nblk + h))]
    args = [qkv, qkv, qkv]
    past = 0
    if cache is not None:
        k_c, v_c = cache
        past = k_c.shape[1]
        in_specs += [pl.BlockSpec((None, past, LANES), lambda b, h, i: (b, 0, h))] * 2
        args += [k_c, v_c]
    sub_scale = None
    if diff is not None:
        lam_row, g_sub, sub_scale = diff
        in_specs += [pl.BlockSpec((1, LANES), lambda b, h, i: (0, 0))] * 2
        args += [lam_row, g_sub]
    return pl.pallas_call(
        functools.partial(_attn_kernel, l1=seq_len, l2=past, diff=diff is not None, sub_scale=sub_scale),
        grid=(n_seq, nblk, qb),
        in_specs=in_specs,
        out_specs=pl.BlockSpec((tq, LANES), lambda b, h, i: (b * qb + i, h)),
        out_shape=jax.ShapeDtypeStruct((n_seq * seq_len, D_MODEL), F32),
        scratch_shapes=[pltpu.VMEM((seq_len + past, LANES), BF16), pltpu.VMEM((seq_len + past, 2 * LANES), BF16)],
        name="attention",
    )(*args)


def _na_bias_table(rel_bias, rows):
    w = GRID_W
    cc = np.arange(w)
    col_start = np.clip(cc - NA_WIN_C // 2, 0, w - NA_WIN_C)
    col_ok = (cc[None, :] >= col_start[:, None]) & (cc[None, :] < col_start[:, None] + NA_WIN_C)
    d_col = np.clip(cc[None, :] - cc[:, None], 1 - NA_WIN_C, NA_WIN_C - 1) + NA_WIN_C - 1
    n_dr = 2 * NA_WIN_R - 1
    tab = jnp.where(jnp.asarray(col_ok)[None, None], rel_bias.astype(F32)[:, :, d_col], NEG_INF)
    tab = jnp.concatenate([tab, jnp.full_like(tab[:, :1], NEG_INF)], axis=1)
    dr_idx = np.full((3, NA_QROWS, NA_UROWS), n_dr, np.int32)
    nblocks = rows // NA_QROWS
    for var, rb in enumerate((0, 1, nblocks - 1)):
        us = int(np.clip(NA_QROWS * rb - NA_WIN_R // 2, 0, rows - NA_UROWS))
        for a in range(NA_QROWS):
            r = NA_QROWS * rb + a
            ws = int(np.clip(r - NA_WIN_R // 2, 0, rows - NA_WIN_R))
            for kr in range(NA_UROWS):
                if ws <= us + kr < ws + NA_WIN_R:
                    dr_idx[var, a, kr] = us + kr - r + NA_WIN_R - 1
    big = tab[:, dr_idx]
    return jnp.transpose(big, (0, 1, 2, 4, 3, 5)).reshape(rel_bias.shape[0], 3, NA_QROWS * w, NA_UROWS * w)


def _na_latent_kernel(q_ref, k_ref, v_ref, kc_ref, vc_ref, bias_ref, o_ref, kb_ref, vb_ref, *, rows):
    tq, tu = NA_QROWS * GRID_W, NA_UROWS * GRID_W
    nblocks = rows // NA_QROWS
    kb_ref[...] = k_ref[...].astype(BF16)
    vb_ref[...] = v_ref[...].astype(BF16)
    kc, vc = kc_ref[...].astype(BF16), vc_ref[...].astype(BF16)

    def block(rb, carry):
        us = jnp.clip(NA_QROWS * rb - NA_WIN_R // 2, 0, rows - NA_UROWS)
        var = jnp.where(rb == 0, 0, jnp.where(rb == nblocks - 1, 2, 1))
        q0 = pl.multiple_of(rb * tq, tq)
        k0 = pl.multiple_of(us * GRID_W, GRID_W)
        low, q_lo, q_hi = _split_maps(q_ref[pl.ds(q0, tq), :])
        ku, vu = kb_ref[pl.ds(k0, tu), :], vb_ref[pl.ds(k0, tu), :]
        outs = []
        for hd, qm in enumerate((q_lo, q_hi)):
            s_loc = _scores(qm, ku) + bias_ref[hd, var]
            s_ctx = _scores(qm, kc)
            m = jnp.maximum(jnp.max(s_loc, axis=-1, keepdims=True), jnp.max(s_ctx, axis=-1, keepdims=True))
            p_loc, p_ctx = jnp.exp(s_loc - m), jnp.exp(s_ctx - m)
            l = jnp.sum(p_loc, axis=-1, keepdims=True) + jnp.sum(p_ctx, axis=-1, keepdims=True)
            acc = (jnp.dot(p_loc.astype(BF16), vu, preferred_element_type=F32)
                   + jnp.dot(p_ctx.astype(BF16), vc, preferred_element_type=F32))
            outs.append(acc / l)
        o_ref[pl.ds(q0, tq), :] = jnp.where(low, outs[0], outs[1])
        return carry

    lax.fori_loop(0, nblocks, block, 0)


def _na_latent(qkv, row0, n_seq, seq_len, k_ctx, v_ctx, bias_tab):
    nblk = D_MODEL // LANES
    rows = seq_len // GRID_W
    assert row0 % seq_len == 0
    seq0 = row0 // seq_len
    past = k_ctx.shape[1]
    tq, tu = NA_QROWS * GRID_W, NA_UROWS * GRID_W
    col = lambda off: pl.BlockSpec((seq_len, LANES), lambda b, h: (seq0 + b, off + h))
    return pl.pallas_call(
        functools.partial(_na_latent_kernel, rows=rows),
        grid=(n_seq, nblk),
        in_specs=[col(0), col(nblk), col(2 * nblk),
                  pl.BlockSpec((None, past, LANES), lambda b, h: (b, 0, h)),
                  pl.BlockSpec((None, past, LANES), lambda b, h: (b, 0, h)),
                  pl.BlockSpec((2, 3, tq, tu), lambda b, h: (h, 0, 0, 0))],
        out_specs=pl.BlockSpec((seq_len, LANES), lambda b, h: (b, h)),
        out_shape=jax.ShapeDtypeStruct((n_seq * seq_len, D_MODEL), F32),
        scratch_shapes=[pltpu.VMEM((seq_len, LANES), BF16), pltpu.VMEM((seq_len, LANES), BF16)],
        name="na_latent",
    )(qkv, qkv, qkv, k_ctx, v_ctx, bias_tab)


def _rope_tables(t_ctx, lat_len, n_lat_seq):
    pos = np.arange(lat_len)
    n_freq = HEAD_DIM // 4
    inv = jnp.power(ROPE_BASE, -jnp.arange(n_freq, dtype=F32) / n_freq)
    ar = jnp.asarray(pos // GRID_W, F32)[:, None] * inv
    ac = jnp.asarray(pos % GRID_W, F32)[:, None] * inv
    cos = jnp.concatenate([jnp.cos(ar), jnp.cos(ar), jnp.cos(ac), jnp.cos(ac)], axis=-1)
    sin = jnp.concatenate([jnp.sin(ar), jnp.sin(ar), jnp.sin(ac), jnp.sin(ac)], axis=-1)
    first = (np.arange(HEAD_DIM) % (2 * n_freq)) < n_freq
    sin_up = jnp.where(jnp.asarray(first)[None, :], -sin, 0.0)
    sin_dn = jnp.where(jnp.asarray(first)[None, :], 0.0, sin)

    def full(tab, ctx_value):
        tab = jnp.tile(tab, (n_lat_seq, LANES // HEAD_DIM))
        return jnp.concatenate([jnp.full((t_ctx, LANES), ctx_value, F32), tab], axis=0)

    return full(cos, 1.0), full(sin_up, 0.0), full(sin_dn, 0.0)


def kernel(x_prompt, x_sample, state_l0_re, state_l0_im, cache_l1_k, cache_l1_v, cache_l2_k, cache_l2_v, state_l3_re, state_l3_im, c, c_ctx, ada_w, ada_b, norm_g, s5_a_re, s5_a_im, s5_b_re, s5_b_im, s5_c_re, s5_c_im, s5_log_step, s5_d, s5_glu_a, s5_glu_b, na_wqkv, na_wo, na_rel_bias, da_wqkv, da_wo, da_lq1, da_lk1, da_lq2, da_lk2, da_subln_g, ff_wg, ff_wu, ff_wd, moe_router, moe_wg, moe_wu, moe_wd):
    n_ctx_seq, ctx_len, d = x_prompt.shape
    n_lat_seq, lat_len, _ = x_sample.shape
    past = cache_l1_k.shape[1]
    depth = ada_w.shape[0]
    t_ctx = n_ctx_seq * ctx_len
    rows_per_group = lat_len
    assert d == D_MODEL and t_ctx == rows_per_group and n_lat_seq == 2
    caches = [(state_l0_re, state_l0_im), (cache_l1_k, cache_l1_v),
              (cache_l2_k, cache_l2_v), (state_l3_re, state_l3_im)]

    x = jnp.concatenate([x_prompt.reshape(t_ctx, d), x_sample.reshape(n_lat_seq * lat_len, d)], axis=0)
    cond8 = jnp.concatenate([c_ctx[None, :], c, jnp.zeros((8 - 1 - n_lat_seq, d), F32)], axis=0)
    m = _modulation(cond8, ada_w, ada_b)
    mods = jnp.pad(m[:, :1 + n_lat_seq].reshape(depth, 1 + n_lat_seq, 6, d), ((0, 0), (0, 0), (0, 2), (0, 0)))
    rope = _rope_tables(t_ctx, lat_len, n_lat_seq)

    new_state = []
    for i in range(depth):
        kind, j = i % 3, i // 3
        mod = mods[i]
        g = norm_g[i].astype(F32).reshape(4, 1, d)
        if kind == 0:
            params = (s5_a_re[j], s5_a_im[j], s5_b_re[j], s5_b_im[j], s5_c_re[j], s5_c_im[j],
                      s5_log_step[j], s5_d[j], s5_glu_a[j], s5_glu_b[j])
            x, st_re, st_im = _s5_layer(x, mod, g[0], g[1], params, caches[i][0], caches[i][1],
                                        n_ctx_seq, ctx_len, n_lat_seq, lat_len, rows_per_group)
            new_state += [st_re, st_im]
        elif kind == 1:
            qkv = _pre_linear(x, mod, g[0], na_wqkv[j].astype(BF16), rows_per_group)
            o_ctx = _attention(qkv, n_ctx_seq, ctx_len, 0, tq=ctx_len)
            bias_tab = _na_bias_table(na_rel_bias[j], lat_len // GRID_W)
            o_lat = _na_latent(qkv, t_ctx, n_lat_seq, lat_len, caches[i][0].reshape(n_lat_seq, past, d),
                               caches[i][1].reshape(n_lat_seq, past, d), bias_tab)
            x = _linear_post(jnp.concatenate([o_ctx, o_lat], axis=0), na_wo[j].astype(BF16), x, mod, g[1],
                             rows_per_group)
            new_state += [qkv[:t_ctx, d:2 * d].reshape(n_ctx_seq, ctx_len, NA_HEADS, HEAD_DIM),
                          qkv[:t_ctx, 2 * d:].reshape(n_ctx_seq, ctx_len, NA_HEADS, HEAD_DIM)]
        else:
            lam_init = 0.8 - 0.6 * math.exp(-0.3 * i)
            lam = (jnp.exp(jnp.sum(da_lq1[j].astype(F32) * da_lk1[j].astype(F32)))
                   - jnp.exp(jnp.sum(da_lq2[j].astype(F32) * da_lk2[j].astype(F32))) + lam_init)
            diff = (jnp.full((1, LANES), lam, F32), da_subln_g[j].astype(F32).reshape(1, LANES), 1.0 - lam_init)
            qkv = _pre_linear(x, mod, g[0], da_wqkv[j].astype(BF16), rows_per_group, rope=rope)
            o_ctx = _attention(qkv, n_ctx_seq, ctx_len, 0, tq=ctx_len, diff=diff)
            o_lat = _attention(qkv, n_lat_seq, lat_len, t_ctx, tq=256,
                               cache=(caches[i][0].reshape(n_lat_seq, past, d), caches[i][1].reshape(n_lat_seq, past, d)),
                               diff=diff)
            x = _linear_post(jnp.concatenate([o_ctx, o_lat], axis=0), da_wo[j].astype(BF16), x, mod, g[1],
                             rows_per_group)
            new_state += [qkv[:t_ctx, d:2 * d].reshape(n_ctx_seq, ctx_len, DA_HEADS, 2, HEAD_DIM),
                          qkv[:t_ctx, 2 * d:].reshape(n_ctx_seq, ctx_len, DA_HEADS, 2 * HEAD_DIM)]
        jf = i // 2
        if i % 2 == 0:
            x = _ffn_dense(x, mod, g[2], g[3], ff_wg[jf].astype(BF16), ff_wu[jf].astype(BF16),
                           ff_wd[jf].astype(BF16), rows_per_group, tf=ff_wg.shape[2] // 2)
        else:
            x = _moe(x, mod, g[2], g[3], moe_router[jf].astype(F32), moe_wg[jf].astype(BF16),
                     moe_wu[jf].astype(BF16), moe_wd[jf].astype(BF16), rows_per_group)
    y_prompt = x[:t_ctx].reshape(n_ctx_seq, ctx_len, d)
    y_sample = x[t_ctx:].reshape(n_lat_seq, lat_len, d)
    return (y_prompt, y_sample, *new_state)
```

```python
import functools
import math

import numpy as np
import jax
import jax.numpy as jnp
from jax import lax
from jax.experimental import pallas as pl
from jax.experimental.pallas import tpu as pltpu

F32 = jnp.float32
BF16 = jnp.bfloat16
HIGHEST = lax.Precision.HIGHEST

D_MODEL = 1024
GRID_W = 64
S5_GROUP = 16
S5_GROUPS = D_MODEL // S5_GROUP
S5_STATE = 64
NA_HEADS = 16
NA_WIN_R = 8
NA_WIN_C = 16
DA_HEADS = 8
HEAD_DIM = 64
ROPE_BASE = 10000.0
N_EXPERTS = 8
NORM_EPS = 1e-6
NEG_INF = -1e30
LANES = 128
SUBLANES = 8
ATTN_SCALE = HEAD_DIM ** -0.5

ROW_TILE = 512
NA_QROWS = 4
NA_UROWS = NA_QROWS + NA_WIN_R

ROW_CHUNKS = D_MODEL // LANES
S5_CHUNK = SUBLANES
S5_BLOCKS = D_MODEL // LANES
S5_BLOCK_GROUPS = LANES // S5_GROUP
S5_BSTATE = S5_BLOCK_GROUPS * S5_STATE


def _rms(x, g):
    return x * lax.rsqrt(jnp.mean(x * x, axis=-1, keepdims=True) + NORM_EPS) * g


def _pre(x, mod, g, shift_row):
    return _rms(x, g) * (1.0 + mod[shift_row + 1:shift_row + 2, :]) + mod[shift_row:shift_row + 1, :]


def _post(x, y, mod, g, gate_row):
    return x + mod[gate_row:gate_row + 1, :] * _rms(y, g)


def _silu(x):
    return x * jax.nn.sigmoid(x)


def _row_specs(tm, rows_per_group):
    x_spec = pl.BlockSpec((tm, D_MODEL), lambda i, *_: (i, 0))
    mod_spec = pl.BlockSpec((None, 8, D_MODEL), lambda i, *_: ((i * tm) // rows_per_group, 0, 0))
    g_spec = pl.BlockSpec((1, D_MODEL), lambda i, *_: (0, 0))
    return x_spec, mod_spec, g_spec


def _mod_kernel(c_ref, w_ref, b_ref, o_ref):
    s = _silu(c_ref[...])
    o_ref[...] = jnp.dot(s, w_ref[...], precision=HIGHEST, preferred_element_type=F32) + b_ref[...]


def _modulation(cond8, ada_w, ada_b):
    depth, _, n = ada_w.shape
    tn = n // 4
    return pl.pallas_call(
        _mod_kernel,
        grid=(depth, n // tn),
        in_specs=[pl.BlockSpec((8, D_MODEL), lambda l, j: (0, 0)),
                  pl.BlockSpec((None, D_MODEL, tn), lambda l, j: (l, 0, j)),
                  pl.BlockSpec((None, 1, tn), lambda l, j: (l, 0, j))],
        out_specs=pl.BlockSpec((None, 8, tn), lambda l, j: (l, 0, j)),
        out_shape=jax.ShapeDtypeStruct((depth, 8, n), F32),
        name="modulation",
    )(cond8, ada_w, ada_b.reshape(depth, 1, n))


def _pre_linear_kernel(x_ref, mod_ref, g_ref, w_ref, *rest, rope_tiles):
    if rope_tiles:
        cos_ref, sin_up_ref, sin_dn_ref, o_ref, h_ref = rest
    else:
        o_ref, h_ref = rest
    j = pl.program_id(1)

    @pl.when(j == 0)
    def _():
        h_ref[...] = _pre(x_ref[...], mod_ref[...], g_ref[...], 0).astype(BF16)

    y = jnp.dot(h_ref[...], w_ref[...], preferred_element_type=F32)
    if not rope_tiles:
        o_ref[...] = y
        return

    @pl.when(j < rope_tiles)
    def _():
        cos, s_up, s_dn = cos_ref[...], sin_up_ref[...], sin_dn_ref[...]
        for s in range(y.shape[1] // LANES):
            ys = y[:, s * LANES:(s + 1) * LANES]
            o_ref[:, s * LANES:(s + 1) * LANES] = (
                ys * cos + pltpu.roll(ys, LANES - 16, 1) * s_up + pltpu.roll(ys, 16, 1) * s_dn)

    @pl.when(j >= rope_tiles)
    def _():
        o_ref[...] = y


def _pre_linear(x, mod, g, w, rows_per_group, rope=None, tn=512):
    t, n = x.shape[0], w.shape[1]
    tm = ROW_TILE
    x_spec, mod_spec, g_spec = _row_specs(tm, rows_per_group)
    in_specs = [x_spec, mod_spec, g_spec, pl.BlockSpec((D_MODEL, tn), lambda i, j: (0, j))]
    args = [x, mod, g, w]
    rope_tiles = 0
    if rope is not None:
        rope_tiles = (2 * D_MODEL) // tn
        in_specs += [pl.BlockSpec((tm, LANES), lambda i, j: (i, 0))] * 3
        args += list(rope)
    return pl.pallas_call(
        functools.partial(_pre_linear_kernel, rope_tiles=rope_tiles),
        grid=(t // tm, n // tn),
        in_specs=in_specs,
        out_specs=pl.BlockSpec((tm, tn), lambda i, j: (i, j)),
        out_shape=jax.ShapeDtypeStruct((t, n), F32),
        scratch_shapes=[pltpu.VMEM((tm, D_MODEL), BF16)],
        name="pre_linear",
    )(*args)


def _linear_post_kernel(a_ref, w_ref, x_ref, mod_ref, g_ref, o_ref):
    y = jnp.dot(a_ref[...].astype(BF16), w_ref[...], preferred_element_type=F32)
    o_ref[...] = _post(x_ref[...], y, mod_ref[...], g_ref[...], 2)


def _linear_post(a, w, x, mod, g, rows_per_group):
    t = x.shape[0]
    tm = ROW_TILE
    x_spec, mod_spec, g_spec = _row_specs(tm, rows_per_group)
    return pl.pallas_call(
        _linear_post_kernel,
        grid=(t // tm,),
        in_specs=[x_spec, pl.BlockSpec((D_MODEL, D_MODEL), lambda i: (0, 0)), x_spec, mod_spec, g_spec],
        out_specs=x_spec,
        out_shape=jax.ShapeDtypeStruct((t, D_MODEL), F32),
        name="linear_post",
    )(a, w, x, mod, g)


def _ffn_kernel(x_ref, mod_ref, g_pre_ref, g_post_ref, wg_ref, wu_ref, wd_ref, o_ref, h_ref, acc_ref):
    f = pl.program_id(1)

    @pl.when(f == 0)
    def _():
        h_ref[...] = _pre(x_ref[...], mod_ref[...], g_pre_ref[...], 3).astype(BF16)
        acc_ref[...] = jnp.zeros_like(acc_ref)

    h = h_ref[...]
    gate = jnp.dot(h, wg_ref[...], preferred_element_type=F32)
    up = jnp.dot(h, wu_ref[...], preferred_element_type=F32)
    acc_ref[...] += jnp.dot((_silu(gate) * up).astype(BF16), wd_ref[...], preferred_element_type=F32)

    @pl.when(f == pl.num_programs(1) - 1)
    def _():
        o_ref[...] = _post(x_ref[...], acc_ref[...], mod_ref[...], g_post_ref[...], 5)


def _ffn_dense(x, mod, g_pre, g_post, wg, wu, wd, rows_per_group, tf):
    t, d_ff = x.shape[0], wg.shape[1]
    tm = ROW_TILE
    x_spec, mod_spec, g_spec = _row_specs(tm, rows_per_group)
    return pl.pallas_call(
        _ffn_kernel,
        grid=(t // tm, d_ff // tf),
        in_specs=[x_spec, mod_spec, g_spec, g_spec,
                  pl.BlockSpec((D_MODEL, tf), lambda i, f: (0, f)),
                  pl.BlockSpec((D_MODEL, tf), lambda i, f: (0, f)),
                  pl.BlockSpec((tf, D_MODEL), lambda i, f: (f, 0))],
        out_specs=x_spec,
        out_shape=jax.ShapeDtypeStruct((t, D_MODEL), F32),
        scratch_shapes=[pltpu.VMEM((tm, D_MODEL), BF16), pltpu.VMEM((tm, D_MODEL), F32)],
        name="ffn_dense",
    )(x, mod, g_pre, g_post, wg, wu, wd)


def _store_row_tiles(ref, value):
    rows = value.shape[0]
    for s in range(ROW_CHUNKS):
        ref[pl.ds(s, rows, stride=ROW_CHUNKS), :] = value[:, s * LANES:(s + 1) * LANES]


def _load_row_tiles(ref, first_row, rows):
    base = first_row * ROW_CHUNKS
    return jnp.concatenate([ref[pl.ds(base + s, rows, stride=ROW_CHUNKS), :] for s in range(ROW_CHUNKS)], axis=1)


def _row_copy(src_hbm, dst_ref, sem, src_row, dst_row):
    return pltpu.make_async_copy(
        src_hbm.at[pl.ds(pl.multiple_of(src_row * ROW_CHUNKS, ROW_CHUNKS), ROW_CHUNKS), :],
        dst_ref.at[pl.ds(pl.multiple_of(dst_row * ROW_CHUNKS, ROW_CHUNKS), ROW_CHUNKS), :], sem)


def _start_rows(src_hbm, idx_ref, idx_base, dst_ref, dst_base, sem, n):
    def body(i, carry):
        for prio in range(2):
            r = 2 * i + prio
            _row_copy(src_hbm, dst_ref, sem, idx_ref[idx_base + r], dst_base + r).start(priority=prio)
        return carry
    lax.fori_loop(0, n // 2, body, 0, unroll=4)


def _wait_rows(src_hbm, dst_ref, dst_base, sem, n):
    def body(r, carry):
        _row_copy(src_hbm, dst_ref, sem, 0, dst_base + r).wait()
        return carry
    lax.fori_loop(0, n, body, 0, unroll=8)


def _router_kernel(x_ref, mod_ref, g_ref, wr_ref, h_ref, r_ref):
    h = _pre(x_ref[...], mod_ref[...], g_ref[...], 3)
    _store_row_tiles(h_ref, h)
    logits = jnp.dot(h, wr_ref[...], precision=HIGHEST, preferred_element_type=F32)
    lane = lax.broadcasted_iota(jnp.int32, logits.shape, 1)
    lg = jnp.where(lane < N_EXPERTS, logits, -jnp.inf)
    m1 = jnp.max(lg, axis=-1, keepdims=True)
    i1 = jnp.min(jnp.where(lg == m1, lane, LANES), axis=-1, keepdims=True)
    lg2 = jnp.where(lane == i1, -jnp.inf, lg)
    m2 = jnp.max(lg2, axis=-1, keepdims=True)
    i2 = jnp.min(jnp.where(lg2 == m2, lane, LANES), axis=-1, keepdims=True)
    e = jnp.exp(m2 - m1)
    w1 = 1.0 / (1.0 + e)
    w2 = e / (1.0 + e)
    r_ref[...] = jnp.where(lane == 0, i1.astype(F32),
                           jnp.where(lane == 1, i2.astype(F32),
                                     jnp.where(lane == 2, w1, jnp.where(lane == 3, w2, 0.0))))


def _router(x, mod, g, w_router, rows_per_group):
    t = x.shape[0]
    tm = ROW_TILE
    x_spec, mod_spec, g_spec = _row_specs(tm, rows_per_group)
    wr = jnp.pad(w_router, ((0, 0), (0, LANES - N_EXPERTS)))
    return pl.pallas_call(
        _router_kernel,
        grid=(t // tm,),
        in_specs=[x_spec, mod_spec, g_spec, pl.BlockSpec((D_MODEL, LANES), lambda i: (0, 0))],
        out_specs=[pl.BlockSpec((tm * ROW_CHUNKS, LANES), lambda i: (i, 0)),
                   pl.BlockSpec((tm, LANES), lambda i: (i, 0))],
        out_shape=[jax.ShapeDtypeStruct((t * ROW_CHUNKS, LANES), F32), jax.ShapeDtypeStruct((t, LANES), F32)],
        name="moe_router",
    )(x, mod, g, wr)


def _moe_ffn_kernel(te_ref, nvalid_ref, src_ref, h_hbm, wg_ref, wu_ref, wd_ref, o_ref,
                    xg_ref, xb_ref, acc_ref, sem, *, tm):
    m, f = pl.program_id(0), pl.program_id(1)
    n_valid = nvalid_ref[0]
    last = f == pl.num_programs(1) - 1
    valid = m < n_valid

    @pl.when(valid & (f == 0))
    def _():
        slot = m % 2

        @pl.when(m == 0)
        def _():
            _start_rows(h_hbm, src_ref, 0, xg_ref, 0, sem.at[0], tm)

        _wait_rows(h_hbm, xg_ref, slot * tm, sem.at[slot], tm)

        @pl.when(m + 1 < n_valid)
        def _():
            _start_rows(h_hbm, src_ref, (m + 1) * tm, xg_ref, (1 - slot) * tm, sem.at[1 - slot], tm)

        xb_ref[...] = _load_row_tiles(xg_ref, slot * tm, tm).astype(BF16)
        acc_ref[...] = jnp.zeros_like(acc_ref)

    @pl.when(valid)
    def _():
        h = xb_ref[...]
        gate = jnp.dot(h, wg_ref[...], preferred_element_type=F32)
        up = jnp.dot(h, wu_ref[...], preferred_element_type=F32)
        acc_ref[...] += jnp.dot((_silu(gate) * up).astype(BF16), wd_ref[...], preferred_element_type=F32)

        @pl.when(last)
        def _():
            _store_row_tiles(o_ref, acc_ref[...])

    @pl.when(jnp.logical_not(valid) & last)
    def _():
        o_ref[...] = jnp.zeros_like(o_ref)


def _moe_ffn(h_tiles, src, tile_expert, n_valid, wg, wu, wd, tm, tf):
    n_tiles, d_ff = tile_expert.shape[0], wg.shape[2]
    nf = d_ff // tf

    def m_eff(m, nv):
        return jnp.minimum(m, nv[0] - 1)

    def f_eff(m, f, nv):
        return jnp.where(m < nv[0], f, nf - 1)

    return pl.pallas_call(
        functools.partial(_moe_ffn_kernel, tm=tm),
        grid_spec=pltpu.PrefetchScalarGridSpec(
            num_scalar_prefetch=3,
            grid=(n_tiles, nf),
            in_specs=[pl.BlockSpec(memory_space=pl.ANY),
                      pl.BlockSpec((None, D_MODEL, tf), lambda m, f, te, nv, src: (te[m_eff(m, nv)], 0, f_eff(m, f, nv))),
                      pl.BlockSpec((None, D_MODEL, tf), lambda m, f, te, nv, src: (te[m_eff(m, nv)], 0, f_eff(m, f, nv))),
                      pl.BlockSpec((None, tf, D_MODEL), lambda m, f, te, nv, src: (te[m_eff(m, nv)], f_eff(m, f, nv), 0))],
            out_specs=pl.BlockSpec((tm * ROW_CHUNKS, LANES), lambda m, f, te, nv, src: (m, 0)),
            scratch_shapes=[pltpu.VMEM((2 * tm * ROW_CHUNKS, LANES), F32), pltpu.VMEM((tm, D_MODEL), BF16),
                            pltpu.VMEM((tm, D_MODEL), F32), pltpu.SemaphoreType.DMA((2,))]),
        out_shape=jax.ShapeDtypeStruct((n_tiles * tm * ROW_CHUNKS, LANES), F32),
        name="moe_ffn",
    )(tile_expert, n_valid, src, h_tiles, wg, wu, wd)


def _moe_post_kernel(slot_ref, y_hbm, r_ref, x_ref, mod_ref, g_ref, o_ref, yg_ref, sem, *, tm, t):
    i = pl.program_id(0)
    buf = i % 2

    def start(step, b):
        for k in range(2):
            _start_rows(y_hbm, slot_ref, k * t + step * tm, yg_ref, (2 * b + k) * tm, sem.at[b], tm)

    @pl.when(i == 0)
    def _():
        start(0, 0)

    for k in range(2):
        _wait_rows(y_hbm, yg_ref, (2 * buf + k) * tm, sem.at[buf], tm)

    @pl.when(i + 1 < pl.num_programs(0))
    def _():
        start(i + 1, 1 - buf)

    r = r_ref[...]
    y = (r[:, 2:3] * _load_row_tiles(yg_ref, 2 * buf * tm, tm)
         + r[:, 3:4] * _load_row_tiles(yg_ref, (2 * buf + 1) * tm, tm))
    o_ref[...] = _post(x_ref[...], y, mod_ref[...], g_ref[...], 5)


def _moe_post(y_tiles, slot, route, x, mod, g, rows_per_group):
    t = x.shape[0]
    tm = ROW_TILE
    x_spec, mod_spec, g_spec = _row_specs(tm, rows_per_group)
    return pl.pallas_call(
        functools.partial(_moe_post_kernel, tm=tm, t=t),
        grid_spec=pltpu.PrefetchScalarGridSpec(
            num_scalar_prefetch=1,
            grid=(t // tm,),
            in_specs=[pl.BlockSpec(memory_space=pl.ANY), pl.BlockSpec((tm, LANES), lambda i, *_: (i, 0)),
                      x_spec, mod_spec, g_spec],
            out_specs=x_spec,
            scratch_shapes=[pltpu.VMEM((4 * tm * ROW_CHUNKS, LANES), F32), pltpu.SemaphoreType.DMA((2,))]),
        out_shape=jax.ShapeDtypeStruct((t, D_MODEL), F32),
        name="moe_post",
    )(slot, y_tiles, route, x, mod, g)


def _moe_slots(e1, e2, tm):
    t = e1.shape[0]
    e_flat = jnp.concatenate([e1, e2])
    onehot = (e_flat[:, None] == jnp.arange(N_EXPERTS, dtype=jnp.int32)[None, :]).astype(jnp.int32)
    csum = jnp.cumsum(onehot, axis=0)
    rank = jnp.sum((csum - onehot) * onehot, axis=1)
    tiles_e = (csum[-1] + tm - 1) // tm
    tile_end = jnp.cumsum(tiles_e)
    slot = jnp.sum(onehot * ((tile_end - tiles_e) * tm)[None, :], axis=1) + rank
    n_tiles = (2 * t) // tm + N_EXPERTS
    tile_expert = jnp.sum(jnp.arange(n_tiles, dtype=jnp.int32)[:, None] >= tile_end[None, :], axis=1)
    tile_expert = jnp.minimum(tile_expert, N_EXPERTS - 1).astype(jnp.int32)
    n_valid = tile_end[-1:].astype(jnp.int32)
    tok = jnp.concatenate([jnp.arange(t, dtype=jnp.int32)] * 2)
    src = jnp.zeros((n_tiles * tm,), jnp.int32).at[slot].set(tok)
    return slot.astype(jnp.int32), src, tile_expert, n_valid


def _moe(x, mod, g_pre, g_post, w_router, wg, wu, wd, rows_per_group, tm=512, tf=512):
    h_tiles, route = _router(x, mod, g_pre, w_router, rows_per_group)
    e1 = route[:, 0].astype(jnp.int32)
    e2 = route[:, 1].astype(jnp.int32)
    slot, src, tile_expert, n_valid = _moe_slots(e1, e2, tm)
    y_tiles = _moe_ffn(h_tiles, src, tile_expert, n_valid, wg, wu, wd, tm, tf)
    return _moe_post(y_tiles, slot, route, x, mod, g_post, rows_per_group)


def _block_diag(x, group_axis, new_axis):
    out = jnp.expand_dims(x, new_axis)
    shape = [1] * out.ndim
    shape[group_axis if group_axis < new_axis else group_axis + 1] = S5_BLOCK_GROUPS
    shape[new_axis] = S5_BLOCK_GROUPS
    return out * jnp.eye(S5_BLOCK_GROUPS, dtype=F32).reshape(shape)


def _s5_matrices(a_re, a_im, b_re, b_im, c_re, c_im, log_step):
    lc, nb, g8, p_, j_ = S5_CHUNK, S5_BLOCKS, S5_BLOCK_GROUPS, S5_STATE, S5_GROUP
    a_re, a_im = a_re.astype(F32), a_im.astype(F32)
    b_re, b_im = b_re.astype(F32), b_im.astype(F32)
    c_re, c_im = c_re.astype(F32), c_im.astype(F32)
    step = jnp.exp(log_step.astype(F32))[..., None]
    zr, zi = a_re * step, a_im * step
    k = jnp.arange(lc + 1, dtype=F32)[:, None, None, None]
    mag = jnp.exp(k * zr[None])
    pr, pi = mag * jnp.cos(k * zi[None]), mag * jnp.sin(k * zi[None])
    den = a_re * a_re + a_im * a_im
    wr = ((pr[1] - 1.0) * a_re + pi[1] * a_im) / den
    wi = (pi[1] * a_re - (pr[1] - 1.0) * a_im) / den
    bbr = wr[..., None] * b_re - wi[..., None] * b_im
    bbi = wr[..., None] * b_im + wi[..., None] * b_re

    clr = c_re[None] * pr[:lc, :, :, None, :] - c_im[None] * pi[:lc, :, :, None, :]
    cli = c_re[None] * pi[:lc, :, :, None, :] + c_im[None] * pr[:lc, :, :, None, :]
    taps = (jnp.einsum('kdgjp,dgpi->kdgji', clr, bbr, precision=HIGHEST)
            - jnp.einsum('kdgjp,dgpi->kdgji', cli, bbi, precision=HIGHEST))
    dif = np.arange(lc)[None, :] - np.arange(lc)[:, None]
    kf = taps[:, 0][np.clip(dif, 0, lc - 1)] * jnp.asarray(dif >= 0, F32)[:, :, None, None, None]
    kb = taps[:, 1][np.clip(-dif, 0, lc - 1)] * jnp.asarray(dif <= 0, F32)[:, :, None, None, None]
    kt = (kf + kb).reshape(lc, lc, nb, g8, j_, j_)
    kt = jnp.transpose(kt, (2, 0, 3, 5, 1, 4))
    kt = _block_diag(kt, 2, 5).reshape(nb, lc * LANES, lc * LANES)

    ms, my = [], []
    for d in range(2):
        e_s = (lc - 1 - np.arange(lc)) if d == 0 else np.arange(lc)
        f_t = (np.arange(lc) + 1) if d == 0 else (lc - np.arange(lc))
        er, ei = pr[e_s, d][..., None], pi[e_s, d][..., None]
        parts = []
        for part in (er * bbr[d][None] - ei * bbi[d][None], er * bbi[d][None] + ei * bbr[d][None]):
            part = jnp.transpose(part.reshape(lc, nb, g8, p_, j_), (1, 0, 2, 4, 3))
            parts.append(_block_diag(part, 2, 4))
        ms.append(jnp.stack(parts, axis=4).reshape(nb, lc * LANES, 2 * S5_BSTATE))
        fr, fi = pr[f_t, d][:, :, None, :], pi[f_t, d][:, :, None, :]
        for part in (c_re[d][None] * fr - c_im[d][None] * fi, -(c_re[d][None] * fi + c_im[d][None] * fr)):
            part = jnp.transpose(part.reshape(lc, nb, g8, j_, p_), (1, 2, 4, 0, 3))
            my.append(_block_diag(part, 1, 4).reshape(nb, S5_BSTATE, lc * LANES))
    my = jnp.concatenate(my, axis=1)
    lam_slab = jnp.concatenate([pr[lc].reshape(2, nb, S5_BSTATE), pi[lc].reshape(2, nb, S5_BSTATE)], axis=-1)
    return kt.astype(BF16), jnp.stack(ms).astype(BF16), my.astype(BF16), lam_slab


def _state_to_slab(st_re, st_im):
    b = st_re.shape[0]
    re = jnp.transpose(st_re.astype(F32), (1, 0, 2, 3)).reshape(2, b, S5_BLOCKS, S5_BSTATE)
    im = jnp.transpose(st_im.astype(F32), (1, 0, 2, 3)).reshape(2, b, S5_BLOCKS, S5_BSTATE)
    return jnp.concatenate([re, im], axis=-1)


def _slab_to_state(slab):
    b = slab.shape[1]
    re = slab[..., :S5_BSTATE].reshape(2, b, S5_GROUPS, S5_STATE)
    im = slab[..., S5_BSTATE:].reshape(2, b, S5_GROUPS, S5_STATE)
    return jnp.transpose(re, (1, 0, 2, 3)), jnp.transpose(im, (1, 0, 2, 3))


def _s5_pre_kernel(x_ref, mod_ref, g_ref, o_ref):
    o_ref[...] = _pre(x_ref[...], mod_ref[...], g_ref[...], 0)


def _s5_pre(x, mod, g, rows_per_group):
    t = x.shape[0]
    tm = ROW_TILE
    x_spec, mod_spec, g_spec = _row_specs(tm, rows_per_group)
    return pl.pallas_call(
        _s5_pre_kernel, grid=(t // tm,), in_specs=[x_spec, mod_spec, g_spec], out_specs=x_spec,
        out_shape=jax.ShapeDtypeStruct((t, D_MODEL), F32), name="s5_pre",
    )(x, mod, g)


def _chunk_rows(ref, rows):
    return jnp.concatenate([ref[pl.ds(t, rows, stride=S5_CHUNK), :] for t in range(S5_CHUNK)], axis=1).astype(BF16)


def _s5_inject_kernel(u_ref, ms_ref, o_ref):
    u = _chunk_rows(u_ref, o_ref.shape[1])
    for d in range(2):
        o_ref[d] = jnp.dot(u, ms_ref[d], preferred_element_type=F32)


def _s5_inject(u, ms, rows):
    t = u.shape[0]
    r = t // S5_CHUNK
    k, n = ms.shape[-2:]
    return pl.pallas_call(
        _s5_inject_kernel,
        grid=(S5_BLOCKS, r // rows),
        in_specs=[pl.BlockSpec((rows * S5_CHUNK, LANES), lambda cb, rb: (rb, cb)),
                  pl.BlockSpec((2, None, k, n), lambda cb, rb: (0, cb, 0, 0))],
        out_specs=pl.BlockSpec((2, rows, n), lambda cb, rb: (0, rb, cb)),
        out_shape=jax.ShapeDtypeStruct((2, r, S5_BLOCKS * n), F32),
        name="s5_inject",
    )(u, ms)


def _s5_scan_kernel(s_ref, lam_ref, h0_ref, hin_ref, hfin_ref, st_ref, *, tb):
    d, kblk = pl.program_id(0), pl.program_id(2)
    half = S5_BSTATE

    @pl.when(kblk == 0)
    def _():
        st_ref[...] = h0_ref[...]

    lam = lam_ref[...]
    lr, li = lam[:, :half], lam[:, half:]

    def body(i, carry):
        hr, hi = carry
        idx = jnp.where(d == 0, i, tb - 1 - i)
        hin_ref[idx, :, :half] = hr
        hin_ref[idx, :, half:] = hi
        s = s_ref[idx]
        return lr * hr - li * hi + s[:, :half], lr * hi + li * hr + s[:, half:]

    st = st_ref[...]
    hr, hi = lax.fori_loop(0, tb, body, (st[:, :half], st[:, half:]))
    st_ref[:, :half] = hr
    st_ref[:, half:] = hi

    @pl.when(kblk == pl.num_programs(2) - 1)
    def _():
        hfin_ref[...] = st_ref[...]


def _s5_scan(s, lam_slab, h0, tb):
    _, n_seq, nc, nb, w = s.shape
    nk = nc // tb

    def blk(d, b, k):
        return (d, b, k + d * (nk - 1 - 2 * k), 0, 0)

    return pl.pallas_call(
        functools.partial(_s5_scan_kernel, tb=tb),
        grid=(2, n_seq, nk),
        in_specs=[pl.BlockSpec((None, None, tb, nb, w), blk),
                  pl.BlockSpec((None, nb, w), lambda d, b, k: (d, 0, 0)),
                  pl.BlockSpec((None, None, nb, w), lambda d, b, k: (d, b, 0, 0))],
        out_specs=[pl.BlockSpec((None, None, tb, nb, w), blk),
                   pl.BlockSpec((None, None, nb, w), lambda d, b, k: (d, b, 0, 0))],
        out_shape=[jax.ShapeDtypeStruct(s.shape, F32), jax.ShapeDtypeStruct((2, n_seq, nb, w), F32)],
        scratch_shapes=[pltpu.VMEM((nb, w), F32)],
        name="s5_scan",
    )(s, lam_slab, h0)


def _s5_readout_kernel(u_ref, kt_ref, hin_ref, my_ref, o_ref):
    rows = hin_ref.shape[1]
    u = _chunk_rows(u_ref, rows)
    h = jnp.concatenate([hin_ref[0], hin_ref[1]], axis=1).astype(BF16)
    y = (jnp.dot(u, kt_ref[...], preferred_element_type=F32)
         + jnp.dot(h, my_ref[...], preferred_element_type=F32))
    for t in range(S5_CHUNK):
        o_ref[pl.ds(t, rows, stride=S5_CHUNK), :] = y[:, t * LANES:(t + 1) * LANES]


def _s5_readout(u, kt, hin, my, rows):
    t = u.shape[0]
    r = t // S5_CHUNK
    w = hin.shape[-1] // S5_BLOCKS
    k = kt.shape[-1]
    return pl.pallas_call(
        _s5_readout_kernel,
        grid=(S5_BLOCKS, r // rows),
        in_specs=[pl.BlockSpec((rows * S5_CHUNK, LANES), lambda cb, rb: (rb, cb)),
                  pl.BlockSpec((None, k, k), lambda cb, rb: (cb, 0, 0)),
                  pl.BlockSpec((2, rows, w), lambda cb, rb: (0, rb, cb)),
                  pl.BlockSpec((None, 2 * w, k), lambda cb, rb: (cb, 0, 0))],
        out_specs=pl.BlockSpec((rows * S5_CHUNK, LANES), lambda cb, rb: (rb, cb)),
        out_shape=jax.ShapeDtypeStruct((t, D_MODEL), F32),
        name="s5_readout",
    )(u, kt, hin, my)


def _s5_glu_post_kernel(x_ref, mod_ref, g_pre_ref, g_post_ref, d_ref, y_ref, wa_ref, wb_ref, o_ref):
    x, mod = x_ref[...], mod_ref[...]
    y = d_ref[...] * _pre(x, mod, g_pre_ref[...], 0) + y_ref[...]
    y = (0.5 * y * (1.0 + jnp.tanh(math.sqrt(2.0 / math.pi) * (y + 0.044715 * (y * y * y))))).astype(BF16)
    out = (jnp.dot(y, wa_ref[...], preferred_element_type=F32)
           * jax.nn.sigmoid(jnp.dot(y, wb_ref[...], preferred_element_type=F32)))
    o_ref[...] = _post(x, out, mod, g_post_ref[...], 2)


def _s5_glu_post(x, mod, g_pre, g_post, d_skip, y, wa, wb, rows_per_group):
    t = x.shape[0]
    tm = ROW_TILE
    x_spec, mod_spec, g_spec = _row_specs(tm, rows_per_group)
    w_spec = pl.BlockSpec((D_MODEL, D_MODEL), lambda i: (0, 0))
    return pl.pallas_call(
        _s5_glu_post_kernel, grid=(t // tm,),
        in_specs=[x_spec, mod_spec, g_spec, g_spec, g_spec, x_spec, w_spec, w_spec],
        out_specs=x_spec, out_shape=jax.ShapeDtypeStruct((t, D_MODEL), F32), name="s5_glu_post",
    )(x, mod, g_pre, g_post, d_skip, y, wa, wb)


def _s5_layer(x, mod, g_pre, g_post, params, st_re, st_im, n_ctx_seq, ctx_len, n_lat_seq, lat_len,
              rows_per_group):
    a_re, a_im, b_re, b_im, c_re, c_im, log_step, d_skip, w_glu_a, w_glu_b = params
    t = x.shape[0]
    lc = S5_CHUNK
    r = t // lc
    rows = r // 2 if r % 2 == 0 else r
    kt, ms, my, lam_slab = _s5_matrices(a_re, a_im, b_re, b_im, c_re, c_im, log_step)

    u = _s5_pre(x, mod, g_pre, rows_per_group)
    w = 2 * S5_BSTATE
    inj = _s5_inject(u, ms, rows).reshape(2, r, S5_BLOCKS, w)
    r_ctx = n_ctx_seq * ctx_len // lc
    s_ctx = inj[:, :r_ctx].reshape(2, n_ctx_seq, ctx_len // lc, S5_BLOCKS, w)
    s_lat = inj[:, r_ctx:].reshape(2, n_lat_seq, lat_len // lc, S5_BLOCKS, w)
    hin_ctx, hfin_ctx = _s5_scan(s_ctx, lam_slab, jnp.zeros((2, n_ctx_seq, S5_BLOCKS, w), F32), tb=ctx_len // lc)
    hin_lat, _ = _s5_scan(s_lat, lam_slab, _state_to_slab(st_re, st_im), tb=min(64, lat_len // lc))
    hin = jnp.concatenate([hin_ctx.reshape(2, r_ctx, S5_BLOCKS * w), hin_lat.reshape(2, r - r_ctx, S5_BLOCKS * w)],
                          axis=1)
    y = _s5_readout(u, kt, hin, my, rows)
    x_new = _s5_glu_post(x, mod, g_pre, g_post, d_skip.reshape(1, D_MODEL).astype(F32), y,
                         w_glu_a.astype(BF16), w_glu_b.astype(BF16), rows_per_group)
    new_re, new_im = _slab_to_state(hfin_ctx)
    return x_new, new_re, new_im


def _split_maps(q):
    lane = lax.broadcasted_iota(jnp.int32, q.shape, 1)
    low = lane < HEAD_DIM
    q = q * ATTN_SCALE
    return low, jnp.where(low, q, 0.0).astype(BF16), jnp.where(low, 0.0, q).astype(BF16)


def _scores(q, k):
    return lax.dot_general(q, k, (((1,), (1,)), ((), ())), preferred_element_type=F32)


def _attn_kernel(*refs, l1, l2, diff, sub_scale):
    q_ref, k1_ref, v1_ref = refs[:3]
    refs = refs[3:]
    if l2:
        k2_ref, v2_ref = refs[:2]
        refs = refs[2:]
    if diff:
        lam_ref, gsub_ref = refs[:2]
        refs = refs[2:]
    o_ref, kb_ref, vb_ref = refs
    tq = q_ref.shape[0]

    @pl.when(pl.program_id(2) == 0)
    def _():
        kb_ref[0:l1, :] = k1_ref[...].astype(BF16)
        vb_ref[0:l1, 0:LANES] = v1_ref[...].astype(BF16)
        if l2:
            kb_ref[l1:l1 + l2, :] = k2_ref[...].astype(BF16)
            vb_ref[l1:l1 + l2, 0:LANES] = v2_ref[...].astype(BF16)
        vb_ref[:, LANES:] = jnp.ones((l1 + l2, LANES), BF16)

    low, q_lo, q_hi = _split_maps(q_ref[...])
    q2 = jnp.concatenate([q_lo, q_hi], axis=0)
    s = _scores(q2, kb_ref[...])
    p = jnp.exp((s - jnp.max(s, axis=-1, keepdims=True)).astype(BF16))
    acc = jnp.dot(p, vb_ref[...], preferred_element_type=F32)
    o = acc[:, :LANES] / acc[:, LANES:]
    if diff:
        o = o[:tq] - lam_ref[...] * o[tq:]
        o_ref[...] = _rms(o, gsub_ref[...]) * sub_scale
    else:
        o_ref[...] = jnp.where(low, o[:tq], o[tq:])


def _attention(qkv, n_seq, seq_len, row0, tq, cache=None, diff=None):
    nblk = D_MODEL // LANES
    qb = seq_len // tq
    assert row0 % seq_len == 0 and seq_len % tq == 0
    seq0 = row0 // seq_len
    in_specs = [pl.BlockSpec((tq, LANES), lambda b, h, i: ((seq0 + b) * qb + i, h)),
                pl.BlockSpec((seq_len, LANES), lambda b, h, i: (seq0 + b, nblk + h)),
                pl.BlockSpec((seq_len, LANES), lambda b, h, i: (seq0 + b, 2 * nblk + h))]
    args = [qkv, qkv, qkv]
    past = 0
    if cache is not None:
        k_c, v_c = cache
        past = k_c.shape[1]
        in_specs += [pl.BlockSpec((None, past, LANES), lambda b, h, i: (b, 0, h))] * 2
        args += [k_c, v_c]
    sub_scale = None
    if diff is not None:
        lam_row, g_sub, sub_scale = diff
        in_specs += [pl.BlockSpec((1, LANES), lambda b, h, i: (0, 0))] * 2
        args += [lam_row, g_sub]
    return pl.pallas_call(
        functools.partial(_attn_kernel, l1=seq_len, l2=past, diff=diff is not None, sub_scale=sub_scale),
        grid=(n_seq, nblk, qb),
        in_specs=in_specs,
        out_specs=pl.BlockSpec((tq, LANES), lambda b, h, i: (b * qb + i, h)),
        out_shape=jax.ShapeDtypeStruct((n_seq * seq_len, D_MODEL), F32),
        scratch_shapes=[pltpu.VMEM((seq_len + past, LANES), BF16), pltpu.VMEM((seq_len + past, 2 * LANES), BF16)],
        name="attention",
    )(*args)


def _na_bias_table(rel_bias, rows):
    w = GRID_W
    cc = np.arange(w)
    col_start = np.clip(cc - NA_WIN_C // 2, 0, w - NA_WIN_C)
    col_ok = (cc[None, :] >= col_start[:, None]) & (cc[None, :] < col_start[:, None] + NA_WIN_C)
    d_col = np.clip(cc[None, :] - cc[:, None], 1 - NA_WIN_C, NA_WIN_C - 1) + NA_WIN_C - 1
    n_dr = 2 * NA_WIN_R - 1
    tab = jnp.where(jnp.asarray(col_ok)[None, None], rel_bias.astype(F32)[:, :, d_col], NEG_INF)
    tab = jnp.concatenate([tab, jnp.full_like(tab[:, :1], NEG_INF)], axis=1)
    dr_idx = np.full((3, NA_QROWS, NA_UROWS), n_dr, np.int32)
    nblocks = rows // NA_QROWS
    for var, rb in enumerate((0, 1, nblocks - 1)):
        us = int(np.clip(NA_QROWS * rb - NA_WIN_R // 2, 0, rows - NA_UROWS))
        for a in range(NA_QROWS):
            r = NA_QROWS * rb + a
            ws = int(np.clip(r - NA_WIN_R // 2, 0, rows - NA_WIN_R))
            for kr in range(NA_UROWS):
                if ws <= us + kr < ws + NA_WIN_R:
                    dr_idx[var, a, kr] = us + kr - r + NA_WIN_R - 1
    big = tab[:, dr_idx]
    return jnp.transpose(big, (0, 1, 2, 4, 3, 5)).reshape(rel_bias.shape[0], 3, NA_QROWS * w, NA_UROWS * w)


def _na_latent_kernel(q_ref, k_ref, v_ref, kc_ref, vc_ref, bias_ref, o_ref, kb_ref, vb_ref, *, rows):
    tq, tu = NA_QROWS * GRID_W, NA_UROWS * GRID_W
    nblocks = rows // NA_QROWS
    kb_ref[...] = k_ref[...].astype(BF16)
    vb_ref[...] = v_ref[...].astype(BF16)
    kc, vc = kc_ref[...].astype(BF16), vc_ref[...].astype(BF16)

    def block(rb, carry):
        us = jnp.clip(NA_QROWS * rb - NA_WIN_R // 2, 0, rows - NA_UROWS)
        var = jnp.where(rb == 0, 0, jnp.where(rb == nblocks - 1, 2, 1))
        q0 = pl.multiple_of(rb * tq, tq)
        k0 = pl.multiple_of(us * GRID_W, GRID_W)
        low, q_lo, q_hi = _split_maps(q_ref[pl.ds(q0, tq), :])
        ku, vu = kb_ref[pl.ds(k0, tu), :], vb_ref[pl.ds(k0, tu), :]
        outs = []
        for hd, qm in enumerate((q_lo, q_hi)):
            s_loc = _scores(qm, ku) + bias_ref[hd, var]
            s_ctx = _scores(qm, kc)
            m = jnp.maximum(jnp.max(s_loc, axis=-1, keepdims=True), jnp.max(s_ctx, axis=-1, keepdims=True))
            p_loc, p_ctx = jnp.exp(s_loc - m), jnp.exp(s_ctx - m)
            l = jnp.sum(p_loc, axis=-1, keepdims=True) + jnp.sum(p_ctx, axis=-1, keepdims=True)
            acc = (jnp.dot(p_loc.astype(BF16), vu, preferred_element_type=F32)
                   + jnp.dot(p_ctx.astype(BF16), vc, preferred_element_type=F32))
            outs.append(acc / l)
        o_ref[pl.ds(q0, tq), :] = jnp.where(low, outs[0], outs[1])
        return carry

    lax.fori_loop(0, nblocks, block, 0)


def _na_latent(qkv, row0, n_seq, seq_len, k_ctx, v_ctx, bias_tab):
    nblk = D_MODEL // LANES
    rows = seq_len // GRID_W
    assert row0 % seq_len == 0
    seq0 = row0 // seq_len
    past = k_ctx.shape[1]
    tq, tu = NA_QROWS * GRID_W, NA_UROWS * GRID_W
    col = lambda off: pl.BlockSpec((seq_len, LANES), lambda b, h: (seq0 + b, off + h))
    return pl.pallas_call(
        functools.partial(_na_latent_kernel, rows=rows),
        grid=(n_seq, nblk),
        in_specs=[col(0), col(nblk), col(2 * nblk),
                  pl.BlockSpec((None, past, LANES), lambda b, h: (b, 0, h)),
                  pl.BlockSpec((None, past, LANES), lambda b, h: (b, 0, h)),
                  pl.BlockSpec((2, 3, tq, tu), lambda b, h: (h, 0, 0, 0))],
        out_specs=pl.BlockSpec((seq_len, LANES), lambda b, h: (b, h)),
        out_shape=jax.ShapeDtypeStruct((n_seq * seq_len, D_MODEL), F32),
        scratch_shapes=[pltpu.VMEM((seq_len, LANES), BF16), pltpu.VMEM((seq_len, LANES), BF16)],
        name="na_latent",
    )(qkv, qkv, qkv, k_ctx, v_ctx, bias_tab)


def _rope_tables(t_ctx, lat_len, n_lat_seq):
    pos = np.arange(lat_len)
    n_freq = HEAD_DIM // 4
    inv = jnp.power(ROPE_BASE, -jnp.arange(n_freq, dtype=F32) / n_freq)
    ar = jnp.asarray(pos // GRID_W, F32)[:, None] * inv
    ac = jnp.asarray(pos % GRID_W, F32)[:, None] * inv
    cos = jnp.concatenate([jnp.cos(ar), jnp.cos(ar), jnp.cos(ac), jnp.cos(ac)], axis=-1)
    sin = jnp.concatenate([jnp.sin(ar), jnp.sin(ar), jnp.sin(ac), jnp.sin(ac)], axis=-1)
    first = (np.arange(HEAD_DIM) % (2 * n_freq)) < n_freq
    sin_up = jnp.where(jnp.asarray(first)[None, :], -sin, 0.0)
    sin_dn = jnp.where(jnp.asarray(first)[None, :], 0.0, sin)

    def full(tab, ctx_value):
        tab = jnp.tile(tab, (n_lat_seq, LANES // HEAD_DIM))
        return jnp.concatenate([jnp.full((t_ctx, LANES), ctx_value, F32), tab], axis=0)

    return full(cos, 1.0), full(sin_up, 0.0), full(sin_dn, 0.0)


def kernel(x_prompt, x_sample, state_l0_re, state_l0_im, cache_l1_k, cache_l1_v, cache_l2_k, cache_l2_v, state_l3_re, state_l3_im, c, c_ctx, ada_w, ada_b, norm_g, s5_a_re, s5_a_im, s5_b_re, s5_b_im, s5_c_re, s5_c_im, s5_log_step, s5_d, s5_glu_a, s5_glu_b, na_wqkv, na_wo, na_rel_bias, da_wqkv, da_wo, da_lq1, da_lk1, da_lq2, da_lk2, da_subln_g, ff_wg, ff_wu, ff_wd, moe_router, moe_wg, moe_wu, moe_wd):
    n_ctx_seq, ctx_len, d = x_prompt.shape
    n_lat_seq, lat_len, _ = x_sample.shape
    past = cache_l1_k.shape[1]
    depth = ada_w.shape[0]
    t_ctx = n_ctx_seq * ctx_len
    rows_per_group = lat_len
    assert d == D_MODEL and t_ctx == rows_per_group and n_lat_seq == 2
    caches = [(state_l0_re, state_l0_im), (cache_l1_k, cache_l1_v),
              (cache_l2_k, cache_l2_v), (state_l3_re, state_l3_im)]

    x = jnp.concatenate([x_prompt.reshape(t_ctx, d), x_sample.reshape(n_lat_seq * lat_len, d)], axis=0)
    cond8 = jnp.concatenate([c_ctx[None, :], c, jnp.zeros((8 - 1 - n_lat_seq, d), F32)], axis=0)
    m = _modulation(cond8, ada_w, ada_b)
    mods = jnp.pad(m[:, :1 + n_lat_seq].reshape(depth, 1 + n_lat_seq, 6, d), ((0, 0), (0, 0), (0, 2), (0, 0)))
    rope = _rope_tables(t_ctx, lat_len, n_lat_seq)

    new_state = []
    for i in range(depth):
        kind, j = i % 3, i // 3
        mod = mods[i]
        g = norm_g[i].astype(F32).reshape(4, 1, d)
        if kind == 0:
            params = (s5_a_re[j], s5_a_im[j], s5_b_re[j], s5_b_im[j], s5_c_re[j], s5_c_im[j],
                      s5_log_step[j], s5_d[j], s5_glu_a[j], s5_glu_b[j])
            x, st_re, st_im = _s5_layer(x, mod, g[0], g[1], params, caches[i][0], caches[i][1],
                                        n_ctx_seq, ctx_len, n_lat_seq, lat_len, rows_per_group)
            new_state += [st_re, st_im]
        elif kind == 1:
            qkv = _pre_linear(x, mod, g[0], na_wqkv[j].astype(BF16), rows_per_group)
            o_ctx = _attention(qkv, n_ctx_seq, ctx_len, 0, tq=ctx_len)
            bias_tab = _na_bias_table(na_rel_bias[j], lat_len // GRID_W)
            o_lat = _na_latent(qkv, t_ctx, n_lat_seq, lat_len, caches[i][0].reshape(n_lat_seq, past, d),
                               caches[i][1].reshape(n_lat_seq, past, d), bias_tab)
            x = _linear_post(jnp.concatenate([o_ctx, o_lat], axis=0), na_wo[j].astype(BF16), x, mod, g[1],
                             rows_per_group)
            new_state += [qkv[:t_ctx, d:2 * d].reshape(n_ctx_seq, ctx_len, NA_HEADS, HEAD_DIM),
                          qkv[:t_ctx, 2 * d:].reshape(n_ctx_seq, ctx_len, NA_HEADS, HEAD_DIM)]
        else:
            lam_init = 0.8 - 0.6 * math.exp(-0.3 * i)
            lam = (jnp.exp(jnp.sum(da_lq1[j].astype(F32) * da_lk1[j].astype(F32)))
                   - jnp.exp(jnp.sum(da_lq2[j].astype(F32) * da_lk2[j].astype(F32))) + lam_init)
            diff = (jnp.full((1, LANES), lam, F32), da_subln_g[j].astype(F32).reshape(1, LANES), 1.0 - lam_init)
            qkv = _pre_linear(x, mod, g[0], da_wqkv[j].astype(BF16), rows_per_group, rope=rope)
            o_ctx = _attention(qkv, n_ctx_seq, ctx_len, 0, tq=ctx_len, diff=diff)
            o_lat = _attention(qkv, n_lat_seq, lat_len, t_ctx, tq=256,
                               cache=(caches[i][0].reshape(n_lat_seq, past, d), caches[i][1].reshape(n_lat_seq, past, d)),
                               diff=diff)
            x = _linear_post(jnp.concatenate([o_ctx, o_lat], axis=0), da_wo[j].astype(BF16), x, mod, g[1],
                             rows_per_group)
            new_state += [qkv[:t_ctx, d:2 * d].reshape(n_ctx_seq, ctx_len, DA_HEADS, 2, HEAD_DIM),
                          qkv[:t_ctx, 2 * d:].reshape(n_ctx_seq, ctx_len, DA_HEADS, 2 * HEAD_DIM)]
        jf = i // 2
        if i % 2 == 0:
            x = _ffn_dense(x, mod, g[2], g[3], ff_wg[jf].astype(BF16), ff_wu[jf].astype(BF16),
                           ff_wd[jf].astype(BF16), rows_per_group, tf=ff_wg.shape[2] // 2)
        else:
            x = _moe(x, mod, g[2], g[3], moe_router[jf].astype(F32), moe_wg[jf].astype(BF16),
                     moe_wu[jf].astype(BF16), moe_wd[jf].astype(BF16), rows_per_group)
    y_prompt = x[:t_ctx].reshape(n_ctx_seq, ctx_len, d)
    y_sample = x[t_ctx:].reshape(n_lat_seq, lat_len, d)
    return (y_prompt, y_sample, *new_state)
```
